```python
import jax, jax.numpy as jnp
from jax import lax
import numpy as np

D_MODEL = 1024
BATCH = 8
SEQ = 4096
DEPTH = 2

N_MIXERS = 2
N_RET = (DEPTH + 1) // 2
N_FOX = DEPTH // 2
D_PLE = 256
D_FF = 2816
EPS = 1e-6
RET_DK = 256
RET_HEADS = D_MODEL // RET_DK
RET_DV = 2 * RET_DK
RET_CHUNK = 128
ROPE_BASE = 10000.0
RET_IN = RET_HEADS * (2 * RET_DK + 2 * RET_DV)
FOX_DH = 64
FOX_HEADS = D_MODEL // FOX_DH
FOX_BLOCK = 128
FOX_IN = 3 * FOX_HEADS * FOX_DH + FOX_HEADS

kernel_name = "hybrid_retention_forgetting_attention_macaron"


def rmsnorm(x, w):
    xf = x.astype(jnp.float32)
    y = xf * lax.rsqrt(jnp.mean(xf * xf, axis=-1, keepdims=True) + EPS)
    return (y * w.astype(jnp.float32)).astype(x.dtype)


def swiglu(x, w_in, w_out):
    g, u = jnp.split(x @ w_in, 2, axis=-1)
    return (jax.nn.silu(g) * u) @ w_out


def rotary(x, pos):
    half = x.shape[-1] // 2
    inv_freq = ROPE_BASE ** (-jnp.arange(half, dtype=jnp.float32) / half)
    ang = pos[:, None] * inv_freq[None, :]
    cos = jnp.cos(ang)[None, :, None, :]
    sin = jnp.sin(ang)[None, :, None, :]
    xf = x.astype(jnp.float32)
    x1, x2 = xf[..., :half], xf[..., half:]
    return jnp.concatenate([x1 * cos - x2 * sin, x1 * sin + x2 * cos], axis=-1).astype(x.dtype)


def retention(h, w_in, gn_w, w_out):
    B, S, _ = h.shape
    H, DK, DV, C = RET_HEADS, RET_DK, RET_DV, RET_CHUNK
    NC = S // C
    q, k, v, g = jnp.split(h @ w_in, [H * DK, 2 * H * DK, 2 * H * DK + H * DV], axis=-1)
    pos = jnp.arange(S, dtype=jnp.float32)
    q = rotary(q.reshape(B, S, H, DK), pos)
    k = rotary(k.reshape(B, S, H, DK), pos) * (DK ** -0.5)
    v = v.reshape(B, S, H, DV)
    log_gamma = jnp.log1p(-jnp.exp2(-5.0 - jnp.arange(H, dtype=jnp.float32)))
    idx = jnp.arange(C, dtype=jnp.float32)
    diff = idx[:, None] - idx[None, :]
    decay_intra = jnp.where(diff[None] >= 0,
                            jnp.exp(log_gamma[:, None, None] * jnp.maximum(diff, 0.0)[None]), 0.0)
    zeta = jnp.exp(log_gamma[:, None] * (C - 1 - idx)[None, :])
    xi = jnp.exp(log_gamma[:, None] * (idx + 1)[None, :]).T
    gamma_c = jnp.exp(log_gamma * C)

    def to_chunks(t):
        return t.reshape(B, NC, C, H, t.shape[-1]).transpose(1, 0, 2, 3, 4)

    def step(state, inp):
        qc, kc, vc = inp
        qf, kf, vf = (t.astype(jnp.float32) for t in (qc, kc, vc))
        s = jnp.einsum('bqhd,bkhd->bhqk', qf, kf) * decay_intra[None]
        inner = jnp.einsum('bhqk,bkhe->bqhe', s, vf)
        cross = jnp.einsum('bqhd,bhde->bqhe', qf, state) * xi[None, :, :, None]
        new_state = gamma_c[None, :, None, None] * state + jnp.einsum('bkhd,hk,bkhe->bhde', kf, zeta, vf)
        return new_state, (inner + cross).astype(qc.dtype)

    state0 = jnp.zeros((B, H, DK, DV), jnp.float32)
    _, o = lax.scan(step, state0, (to_chunks(q), to_chunks(k), to_chunks(v)))
    o = o.transpose(1, 0, 2, 3, 4).reshape(B, S, H, DV)
    o = rmsnorm(o, gn_w)
    y = jax.nn.silu(g) * o.reshape(B, S, H * DV)
    return y @ w_out


def forgetting_attention(h, w_in, b_f, w_out):
    B, S, D = h.shape
    H, DH, Q = FOX_HEADS, FOX_DH, FOX_BLOCK
    NB = S // Q
    q, k, v, fz = jnp.split(h @ w_in, [H * DH, 2 * H * DH, 3 * H * DH], axis=-1)
    q = q.reshape(B, S, H, DH).transpose(0, 2, 1, 3)
    k = k.reshape(B, S, H, DH).transpose(0, 2, 1, 3)
    v = v.reshape(B, S, H, DH).transpose(0, 2, 1, 3)
    log_f = jax.nn.log_sigmoid(fz.astype(jnp.float32) + b_f.astype(jnp.float32))
    c = jnp.cumsum(log_f, axis=1).transpose(0, 2, 1)
    kpos = jnp.arange(S)
    scale = DH ** -0.5

    def block(i):
        start = i * Q
        qb = lax.dynamic_slice_in_dim(q, start, Q, axis=2)
        cb = lax.dynamic_slice_in_dim(c, start, Q, axis=2)
        logits = (jnp.einsum('bhqd,bhkd->bhqk', qb, k).astype(jnp.float32) * scale
                  + cb[..., None] - c[:, :, None, :])
        qpos = start + jnp.arange(Q)
        logits = jnp.where((kpos[None, :] <= qpos[:, None])[None, None], logits, -jnp.inf)
        w = jax.nn.softmax(logits, axis=-1)
        return jnp.einsum('bhqk,bhkd->bhqd', w.astype(v.dtype), v)

    o = lax.map(block, jnp.arange(NB))
    o = o.transpose(1, 0, 3, 2, 4).reshape(B, S, H * DH)
    return o @ w_out


def setup_inputs(seed: int = 0) -> dict:
    key = jax.random.key(seed)
    ks = jax.random.split(key, 14)
    f32 = jnp.float32
    nrm = lambda k, shape, fan: jax.random.normal(k, shape, f32) * (fan ** -0.5)
    return {
        "x": jax.random.normal(ks[0], (BATCH, SEQ, D_MODEL), f32),
        "p": jax.random.normal(ks[1], (DEPTH, BATCH, SEQ, D_PLE), f32),
        "norm_w": 1.0 + 0.02 * jax.random.normal(ks[2], (DEPTH, 4, D_MODEL), f32),
        "ffn_w_in": nrm(ks[3], (DEPTH, 2, D_MODEL, 2 * D_FF), D_MODEL),
        "ffn_w_out": nrm(ks[4], (DEPTH, 2, D_FF, D_MODEL), D_FF),
        "ret_w_in": nrm(ks[5], (N_RET, D_MODEL, RET_IN), D_MODEL),
        "ret_gn_w": 1.0 + 0.02 * jax.random.normal(ks[6], (N_RET, RET_HEADS, RET_DV), f32),
        "ret_w_out": nrm(ks[7], (N_RET, RET_HEADS * RET_DV, D_MODEL), RET_HEADS * RET_DV),
        "fox_w_in": nrm(ks[8], (N_FOX, D_MODEL, FOX_IN), D_MODEL),
        "fox_b_f": jax.random.uniform(ks[9], (N_FOX, FOX_HEADS), f32, 1.0, 4.0),
        "fox_w_out": nrm(ks[10], (N_FOX, D_MODEL, D_MODEL), D_MODEL),
        "ple_w_proj": nrm(ks[11], (DEPTH, D_PLE, D_MODEL), D_PLE),
        "ple_w_gate": nrm(ks[12], (DEPTH, D_MODEL, D_MODEL), D_MODEL),
        "final_norm_w": 1.0 + 0.02 * jax.random.normal(ks[13], (D_MODEL,), f32),
    }


def reference(x, p, norm_w, ffn_w_in, ffn_w_out, ret_w_in, ret_gn_w, ret_w_out,
              fox_w_in, fox_b_f, fox_w_out, ple_w_proj, ple_w_gate, final_norm_w):
    h = x
    for i in range(DEPTH):
        nw = norm_w[i]
        h = h + 0.5 * swiglu(rmsnorm(h, nw[0]), ffn_w_in[i, 0], ffn_w_out[i, 0])
        hn = rmsnorm(h, nw[1])
        j = i // N_MIXERS
        if i % N_MIXERS == 0:
            h = h + retention(hn, ret_w_in[j], ret_gn_w[j], ret_w_out[j])
        else:
            h = h + forgetting_attention(hn, fox_w_in[j], fox_b_f[j], fox_w_out[j])
        h = h + 0.5 * swiglu(rmsnorm(h, nw[2]), ffn_w_in[i, 1], ffn_w_out[i, 1])
        gate = jax.nn.sigmoid(rmsnorm(h, nw[3]) @ ple_w_gate[i])
        h = h + gate * (p[i] @ ple_w_proj[i])
    return rmsnorm(h, final_norm_w)
```

```python
import functools

import jax
import jax.numpy as jnp
from jax import lax
from jax.experimental import pallas as pl
from jax.experimental.pallas import tpu as pltpu

D_MODEL = 1024
D_PLE = 256
D_FF = 2816
EPS = 1e-6
RET_DK = 256
RET_HEADS = D_MODEL // RET_DK
RET_DV = 2 * RET_DK
RET_CHUNK = 128
ROPE_BASE = 10000.0
FOX_DH = 64
FOX_HEADS = D_MODEL // FOX_DH

LANES = 128
VMEM_LIMIT = 56 * 1024 * 1024

F32 = jnp.float32
BF16 = jnp.bfloat16

AUG_ONE = FOX_DH
AUG_C = FOX_DH + 3


def _rms(x, w):
    return x * lax.rsqrt(jnp.mean(x * x, axis=-1, keepdims=True) + EPS) * w


def _dot(a, b):
    return jnp.dot(a, b, preferred_element_type=F32)


def _dot_nt(a, b):
    return lax.dot_general(a, b, (((1,), (1,)), ((), ())), preferred_element_type=F32)


def _dot_tn(a, b):
    return lax.dot_general(a, b, (((0,), (0,)), ((), ())), preferred_element_type=F32)


def _resident(shape, index_map):
    return pl.BlockSpec(shape, index_map, pipeline_mode=pl.Buffered(1))


def _params(*sem):
    return pltpu.CompilerParams(dimension_semantics=sem, vmem_limit_bytes=VMEM_LIMIT)


FFN_TM = 512
FFN_TF = 256


def _ffn_kernel(*refs, ple, final):
    x_ref, nw_ref, win_ref, wout_ref = refs[:4]
    rest = list(refs[4:])
    o_ref = rest.pop()
    x = x_ref[...]
    xn = _rms(x, nw_ref[...]).astype(BF16)
    acc = jnp.zeros_like(x)
    for c in range(D_FF // FFN_TF):
        lo = c * FFN_TF
        g = _dot(xn, win_ref[:, lo:lo + FFN_TF])
        u = _dot(xn, win_ref[:, D_FF + lo:D_FF + lo + FFN_TF])
        a = (g * jax.nn.sigmoid(g) * u).astype(BF16)
        acc = acc + _dot(a, wout_ref[lo:lo + FFN_TF, :])
    h = x + 0.5 * acc
    if ple:
        p_ref, gnw_ref, wgate_ref, wproj_ref = rest[:4]
        rest = rest[4:]
        hn = _rms(h, gnw_ref[...]).astype(BF16)
        gate = jax.nn.sigmoid(_dot(hn, wgate_ref[...]))
        h = h + gate * _dot(p_ref[...].astype(BF16), wproj_ref[...])
    if final:
        h = _rms(h, rest[0][...])
    o_ref[...] = h


def _ffn(h, nw, w_in, w_out, ple=None, final_w=None):
    T = h.shape[0]
    row = lambda t: (t, 0)
    fixed = lambda t: (0, 0)
    in_specs = [
        pl.BlockSpec((FFN_TM, D_MODEL), row),
        _resident((1, D_MODEL), fixed),
        _resident((D_MODEL, 2 * D_FF), fixed),
        _resident((D_FF, D_MODEL), fixed),
    ]
    args = [h, nw, w_in, w_out]
    if ple is not None:
        p, gnw, wgate, wproj = ple
        in_specs += [
            pl.BlockSpec((FFN_TM, D_PLE), row),
            _resident((1, D_MODEL), fixed),
            _resident((D_MODEL, D_MODEL), fixed),
            _resident((D_PLE, D_MODEL), fixed),
        ]
        args += [p, gnw, wgate, wproj]
    if final_w is not None:
        in_specs.append(_resident((1, D_MODEL), fixed))
        args.append(final_w)
    return pl.pallas_call(
        functools.partial(_ffn_kernel, ple=ple is not None, final=final_w is not None),
        grid=(T // FFN_TM,),
        in_specs=in_specs,
        out_specs=pl.BlockSpec((FFN_TM, D_MODEL), row),
        out_shape=jax.ShapeDtypeStruct((T, D_MODEL), F32),
        compiler_params=_params("parallel"),
        name="ffn",
    )(*args)


MIX_TM = 512


def _mix_kernel(a_ref, w_ref, res_ref, o_ref):
    o_ref[...] = res_ref[...] + _dot(a_ref[...], w_ref[...])


def _mix(a, w, res):
    T, K = a.shape
    row = lambda t: (t, 0)
    return pl.pallas_call(
        _mix_kernel,
        grid=(T // MIX_TM,),
        in_specs=[
            pl.BlockSpec((MIX_TM, K), row),
            _resident((K, D_MODEL), lambda t: (0, 0)),
            pl.BlockSpec((MIX_TM, D_MODEL), row),
        ],
        out_specs=pl.BlockSpec((MIX_TM, D_MODEL), row),
        out_shape=jax.ShapeDtypeStruct((T, D_MODEL), F32),
        compiler_params=_params("parallel"),
        name="mix",
    )(a, w, res)


RETP_TM = 512
RET_QK = RET_HEADS * RET_DK
RET_V = RET_HEADS * RET_DV


def _ret_proj_kernel(x_ref, nw_ref, w_ref, cos_ref, sin_ref, q_ref, k_ref, v_ref, g_ref):
    xn = _rms(x_ref[...], nw_ref[...]).astype(BF16)
    cos = cos_ref[...]
    sin = sin_ref[...]
    half = RET_DK // 2

    def rotary(col, out_ref, scale):
        for h in range(RET_HEADS):
            y = _dot(xn, w_ref[:, col + h * RET_DK:col + (h + 1) * RET_DK])
            x1, x2 = y[:, :half], y[:, half:]
            lo = h * RET_DK
            out_ref[:, lo:lo + half] = ((x1 * cos - x2 * sin) * scale).astype(BF16)
            out_ref[:, lo + half:lo + RET_DK] = ((x1 * sin + x2 * cos) * scale).astype(BF16)

    rotary(0, q_ref, 1.0)
    rotary(RET_QK, k_ref, RET_DK ** -0.5)
    for h in range(RET_HEADS):
        lo = h * RET_DV
        v_ref[:, lo:lo + RET_DV] = _dot(xn, w_ref[:, 2 * RET_QK + lo:2 * RET_QK + lo + RET_DV]).astype(BF16)
        g_ref[:, lo:lo + RET_DV] = _dot(
            xn, w_ref[:, 2 * RET_QK + RET_V + lo:2 * RET_QK + RET_V + lo + RET_DV]).astype(BF16)


def _ret_proj(h, nw, w_in, cos, sin, seq):
    T = h.shape[0]
    row = lambda t: (t, 0)
    fixed = lambda t: (0, 0)
    pos = lambda t: (t % (seq // RETP_TM), 0)
    n_in = 2 * RET_QK + 2 * RET_V
    return pl.pallas_call(
        _ret_proj_kernel,
        grid=(T // RETP_TM,),
        in_specs=[
            pl.BlockSpec((RETP_TM, D_MODEL), row),
            _resident((1, D_MODEL), fixed),
            _resident((D_MODEL, n_in), fixed),
            pl.BlockSpec((RETP_TM, RET_DK // 2), pos),
            pl.BlockSpec((RETP_TM, RET_DK // 2), pos),
        ],
        out_specs=[
            pl.BlockSpec((RETP_TM, RET_QK), row),
            pl.BlockSpec((RETP_TM, RET_QK), row),
            pl.BlockSpec((RETP_TM, RET_V), row),
            pl.BlockSpec((RETP_TM, RET_V), row),
        ],
        out_shape=[
            jax.ShapeDtypeStruct((T, RET_QK), BF16),
            jax.ShapeDtypeStruct((T, RET_QK), BF16),
            jax.ShapeDtypeStruct((T, RET_V), BF16),
            jax.ShapeDtypeStruct((T, RET_V), BF16),
        ],
        compiler_params=_params("parallel"),
        name="ret_proj",
    )(h, nw, w_in, cos, sin)


def _ret_core_kernel(gc_ref, q_ref, k_ref, v_ref, g_ref, res_ref, decay_ref, zeta_ref, xi_ref,
                     gn_ref, wo_ref, o_ref, state_ref):
    @pl.when(pl.program_id(1) == 0)
    def _():
        state_ref[...] = jnp.zeros_like(state_ref)

    acc = res_ref[...]
    for h in range(RET_HEADS):
        qh = q_ref[:, h * RET_DK:(h + 1) * RET_DK]
        kh = k_ref[:, h * RET_DK:(h + 1) * RET_DK]
        vh = v_ref[:, h * RET_DV:(h + 1) * RET_DV]
        st = state_ref[h]
        s = _dot_nt(qh, kh) * decay_ref[h]
        inner = _dot(s.astype(BF16), vh)
        cross = _dot(qh, st.astype(BF16)) * xi_ref[h]
        o = inner + cross
        kz = (kh.astype(F32) * zeta_ref[h]).astype(BF16)
        state_ref[h] = gc_ref[h] * st + _dot_tn(kz, vh)
        on = _rms(o, gn_ref[h])
        gh = g_ref[:, h * RET_DV:(h + 1) * RET_DV].astype(F32)
        y = (gh * jax.nn.sigmoid(gh) * on).astype(BF16)
        acc = acc + _dot(y, wo_ref[h * RET_DV:(h + 1) * RET_DV, :])
    o_ref[...] = acc


def _ret_core(q, k, v, g, res, gamma_c, decay, zeta, xi, gn_w, w_out, batch, seq):
    T = q.shape[0]
    C = RET_CHUNK
    nc = seq // C
    row = lambda b, c: (b * nc + c, 0)
    fixed3 = lambda b, c: (0, 0, 0)
    return pl.pallas_call(
        _ret_core_kernel,
        grid=(batch, nc),
        in_specs=[
            pl.BlockSpec(memory_space=pltpu.SMEM),
            pl.BlockSpec((C, RET_QK), row),
            pl.BlockSpec((C, RET_QK), row),
            pl.BlockSpec((C, RET_V), row),
            pl.BlockSpec((C, RET_V), row),
            pl.BlockSpec((C, D_MODEL), row),
            _resident((RET_HEADS, C, C), fixed3),
            _resident((RET_HEADS, C, 1), fixed3),
            _resident((RET_HEADS, C, 1), fixed3),
            _resident((RET_HEADS, 1, RET_DV), fixed3),
            _resident((RET_V, D_MODEL), lambda b, c: (0, 0)),
        ],
        out_specs=pl.BlockSpec((C, D_MODEL), row),
        out_shape=jax.ShapeDtypeStruct((T, D_MODEL), F32),
        scratch_shapes=[pltpu.VMEM((RET_HEADS, RET_DK, RET_DV), F32)],
        compiler_params=_params("parallel", "arbitrary"),
        name="ret_core",
    )(gamma_c, q, k, v, g, res, decay, zeta, xi, gn_w, w_out)


def _retention_tables(seq):
    H, C = RET_HEADS, RET_CHUNK
    half = RET_DK // 2
    pos = jnp.arange(seq, dtype=F32)
    inv_freq = ROPE_BASE ** (-jnp.arange(half, dtype=F32) / half)
    ang = pos[:, None] * inv_freq[None, :]
    log_gamma = jnp.log1p(-jnp.exp2(-5.0 - jnp.arange(H, dtype=F32)))
    idx = jnp.arange(C, dtype=F32)
    diff = idx[:, None] - idx[None, :]
    decay = jnp.where(diff[None] >= 0,
                      jnp.exp(log_gamma[:, None, None] * jnp.maximum(diff, 0.0)[None]), 0.0)
    zeta = jnp.exp(log_gamma[:, None] * (C - 1 - idx)[None, :])[:, :, None]
    xi = jnp.exp(log_gamma[:, None] * (idx + 1)[None, :])[:, :, None]
    gamma_c = jnp.exp(log_gamma * C)
    return jnp.cos(ang), jnp.sin(ang), gamma_c, decay, zeta, xi


FOXP_TM = 512
FOX_BQ = 256
FOX_BK = 256
FOX_QK = FOX_HEADS * LANES


def _cumsum_rows(x):
    n = x.shape[0]
    row = lax.broadcasted_iota(jnp.int32, x.shape, 0)
    shift = 1
    while shift < n:
        x = x + jnp.where(row >= shift, pltpu.roll(x, shift, axis=0), 0.0)
        shift *= 2
    return x


def _split3(c):
    hi = c.astype(BF16).astype(F32)
    r = c - hi
    mid = r.astype(BF16).astype(F32)
    lo = (r - mid).astype(BF16).astype(F32)
    return hi, mid, lo


def _fox_proj_kernel(x_ref, nw_ref, w_ref, wf_ref, bf_ref, place_ref, ones_ref,
                     q_ref, k_ref, v_ref, carry_ref):
    @pl.when(pl.program_id(1) == 0)
    def _():
        carry_ref[...] = jnp.zeros_like(carry_ref)

    xn = _rms(x_ref[0], nw_ref[...]).astype(BF16)
    tm = xn.shape[0]

    z = _dot(xn, wf_ref[...]) + bf_ref[...]
    log_f = jnp.minimum(z, 0.0) - jnp.log(1.0 + jnp.exp(-jnp.abs(z)))
    c = _cumsum_rows(log_f) + carry_ref[...]
    carry_ref[...] = c[tm - 1:tm, :]

    hi, mid, lo = _split3(c)
    lane = lax.broadcasted_iota(jnp.int32, c.shape, 1)
    parts = jnp.where(lane < FOX_HEADS, hi,
                      jnp.where(lane < 2 * FOX_HEADS, pltpu.roll(mid, FOX_HEADS, axis=1),
                                pltpu.roll(lo, 2 * FOX_HEADS, axis=1))).astype(BF16)

    dh = FOX_DH
    nqk = FOX_HEADS * dh
    v_ref[0] = _dot(xn, w_ref[:, 2 * nqk:3 * nqk]).astype(BF16)
    lane_t = lax.broadcasted_iota(jnp.int32, (tm, LANES), 1)
    for which, out_ref, scale in ((0, q_ref, dh ** -0.5), (1, k_ref, 1.0)):
        for pair in range(FOX_HEADS // 2):
            lo_col = which * nqk + pair * LANES
            y = _dot(xn, w_ref[:, lo_col:lo_col + LANES]) * scale
            for hh in range(2):
                h = 2 * pair + hh
                col = which * FOX_QK + h * LANES
                aug = _dot(parts, place_ref[:, col:col + LANES]) + ones_ref[:, col:col + LANES]
                yh = y if hh == 0 else pltpu.roll(y, dh, axis=1)
                out_ref[0, h] = jnp.where(lane_t < dh, yh, aug).astype(BF16)


def _fox_tables():
    H = FOX_HEADS
    rows = jnp.arange(LANES)[:, None]
    cols = jnp.arange(2 * FOX_QK)[None, :]
    which = cols // FOX_QK
    head = (cols % FOX_QK) // LANES
    lane = cols % LANES
    first = jnp.where(which == 0, AUG_C, AUG_ONE)
    part = lane - first
    hit = (part >= 0) & (part < 3) & (rows == part * H + head)
    place = jnp.where(hit, jnp.where(which == 0, 1.0, -1.0), 0.0).astype(BF16)
    one_first = jnp.where(which == 0, AUG_ONE, AUG_C)
    ones = (((lane - one_first) >= 0) & ((lane - one_first) < 3)).astype(F32)
    return place, ones


def _fox_proj(h3, nw, w_qkv, w_f, b_f, place, ones):
    B, S, _ = h3.shape
    H = FOX_HEADS
    fixed = lambda b, t: (0, 0)
    return pl.pallas_call(
        _fox_proj_kernel,
        grid=(B, S // FOXP_TM),
        in_specs=[
            pl.BlockSpec((1, FOXP_TM, D_MODEL), lambda b, t: (b, t, 0)),
            _resident((1, D_MODEL), fixed),
            _resident((D_MODEL, 3 * D_MODEL), fixed),
            _resident((D_MODEL, LANES), fixed),
            _resident((1, LANES), fixed),
            _resident((LANES, 2 * FOX_QK), fixed),
            _resident((1, 2 * FOX_QK), fixed),
        ],
        out_specs=[
            pl.BlockSpec((1, H, FOXP_TM, LANES), lambda b, t: (b, 0, t, 0)),
            pl.BlockSpec((1, H, FOXP_TM, LANES), lambda b, t: (b, 0, t, 0)),
            pl.BlockSpec((1, FOXP_TM, D_MODEL), lambda b, t: (b, t, 0)),
        ],
        out_shape=[
            jax.ShapeDtypeStruct((B, H, S, LANES), BF16),
            jax.ShapeDtypeStruct((B, H, S, LANES), BF16),
            jax.ShapeDtypeStruct((B, S, D_MODEL), BF16),
        ],
        scratch_shapes=[pltpu.VMEM((1, LANES), F32)],
        compiler_params=_params("parallel", "arbitrary"),
        name="fox_proj",
    )(h3, nw, w_qkv, w_f, b_f, place, ones)


def _fox_attn_kernel(q_ref, k_ref, v_ref, o_ref):
    qi = pl.program_id(2)
    bq, bk = FOX_BQ, FOX_BK
    outs = []
    for hh in range(2):
        q = q_ref[0, hh]

        def step(j, carry, masked):
            m, l, acc = carry
            start = pl.multiple_of(j * bk, bk)
            s = _dot_nt(q, k_ref[0, hh, pl.ds(start, bk), :])
            if masked:
                r = lax.broadcasted_iota(jnp.int32, s.shape, 0)
                cidx = lax.broadcasted_iota(jnp.int32, s.shape, 1)
                s = jnp.where(cidx <= r, s, -jnp.inf)
            m_new = jnp.maximum(m, jnp.max(s, axis=-1, keepdims=True))
            alpha = jnp.exp(m - m_new)
            p = jnp.exp(s - m_new)
            l = alpha * l + jnp.sum(p, axis=-1, keepdims=True)
            acc = alpha * acc + _dot(p.astype(BF16), v_ref[0, pl.ds(start, bk), :])
            return m_new, l, acc

        init = (jnp.full((bq, 1), -jnp.inf, F32), jnp.zeros((bq, 1), F32),
                jnp.zeros((bq, LANES), F32))
        carry = lax.fori_loop(0, qi, functools.partial(step, masked=False), init)
        _, l, acc = step(qi, carry, masked=True)
        outs.append(acc / l)
    lane = lax.broadcasted_iota(jnp.int32, (bq, LANES), 1)
    o_ref[0] = jnp.where(lane < FOX_DH, outs[0], outs[1]).astype(BF16)


def _fox_attn(q, k, v):
    B, H, S, _ = q.shape
    assert FOX_BQ == FOX_BK
    return pl.pallas_call(
        _fox_attn_kernel,
        grid=(B, H // 2, S // FOX_BQ),
        in_specs=[
            pl.BlockSpec((1, 2, FOX_BQ, LANES), lambda b, hp, i: (b, hp, i, 0)),
            pl.BlockSpec((1, 2, S, LANES), lambda b, hp, i: (b, hp, 0, 0)),
            pl.BlockSpec((1, S, LANES), lambda b, hp, i: (b, 0, hp)),
        ],
        out_specs=pl.BlockSpec((1, FOX_BQ, LANES), lambda b, hp, i: (b, i, hp)),
        out_shape=jax.ShapeDtypeStruct((B, S, D_MODEL), BF16),
        compiler_params=_params("parallel", "parallel", "arbitrary"),
        name="fox_attn",
    )(q, k, v)


def kernel(x, p, norm_w, ffn_w_in, ffn_w_out, ret_w_in, ret_gn_w, ret_w_out, fox_w_in, fox_b_f,
           fox_w_out, ple_w_proj, ple_w_gate, final_norm_w):
    B, S, D = x.shape
    T = B * S
    depth = norm_w.shape[0]
    ffn_w_in = ffn_w_in.astype(BF16)
    ffn_w_out = ffn_w_out.astype(BF16)
    ret_w_in = ret_w_in.astype(BF16)
    ret_w_out = ret_w_out.astype(BF16)
    fox_w_in_b = fox_w_in.astype(BF16)
    fox_w_out = fox_w_out.astype(BF16)
    ple_w_proj = ple_w_proj.astype(BF16)
    ple_w_gate = ple_w_gate.astype(BF16)
    norm_w = norm_w.reshape(depth, 4, 1, D)
    final_w = final_norm_w.reshape(1, D)
    p = p.reshape(depth, T, D_PLE)

    cos, sin, gamma_c, decay, zeta, xi = _retention_tables(S)
    place, ones = _fox_tables()

    h = x.reshape(T, D)
    for i in range(depth):
        nw = norm_w[i]
        j = i // 2
        h = _ffn(h, nw[0], ffn_w_in[i, 0], ffn_w_out[i, 0])
        if i % 2 == 0:
            q, k, v, g = _ret_proj(h, nw[1], ret_w_in[j], cos, sin, S)
            h = _ret_core(q, k, v, g, h, gamma_c, decay, zeta, xi,
                          ret_gn_w[j].reshape(RET_HEADS, 1, RET_DV), ret_w_out[j], B, S)
        else:
            nqkv = 3 * D
            w_f = jnp.pad(fox_w_in_b[j][:, nqkv:], ((0, 0), (0, LANES - FOX_HEADS)))
            b_f = jnp.pad(fox_b_f[j], (0, LANES - FOX_HEADS)).reshape(1, LANES)
            qa, ka, v = _fox_proj(h.reshape(B, S, D), nw[1], fox_w_in_b[j][:, :nqkv], w_f, b_f,
                                  place, ones)
            o = _fox_attn(qa, ka, v)
            h = _mix(o.reshape(T, D), fox_w_out[j], h)
        h = _ffn(h, nw[2], ffn_w_in[i, 1], ffn_w_out[i, 1],
                 ple=(p[i], nw[3], ple_w_gate[i], ple_w_proj[i]),
                 final_w=final_w if i == depth - 1 else None)
    return h.reshape(B, S, D)
```

```python
import functools

import jax
import jax.numpy as jnp
from jax import lax
from jax.experimental import pallas as pl
from jax.experimental.pallas import tpu as pltpu

D_MODEL = 1024
D_PLE = 256
D_FF = 2816
EPS = 1e-6
RET_DK = 256
RET_HEADS = D_MODEL // RET_DK
RET_DV = 2 * RET_DK
RET_CHUNK = 128
ROPE_BASE = 10000.0
FOX_DH = 64
FOX_HEADS = D_MODEL // FOX_DH

LANES = 128
VMEM_LIMIT = 56 * 1024 * 1024

F32 = jnp.float32
BF16 = jnp.bfloat16

AUG_ONE = FOX_DH
AUG_C = FOX_DH + 3


def _rms(x, w):
    return x * lax.rsqrt(jnp.mean(x * x, axis=-1, keepdims=True) + EPS) * w


def _dot(a, b):
    return jnp.dot(a, b, preferred_element_type=F32)


def _dot_nt(a, b):
    return lax.dot_general(a, b, (((1,), (1,)), ((), ())), preferred_element_type=F32)


def _dot_tn(a, b):
    return lax.dot_general(a, b, (((0,), (0,)), ((), ())), preferred_element_type=F32)


def _resident(shape, index_map):
    return pl.BlockSpec(shape, index_map, pipeline_mode=pl.Buffered(1))


def _params(*sem):
    return pltpu.CompilerParams(dimension_semantics=sem, vmem_limit_bytes=VMEM_LIMIT)


FFN_TM = 512
FFN_TF = 256


def _ffn_kernel(*refs, ple, final):
    x_ref, nw_ref, win_ref, wout_ref = refs[:4]
    rest = list(refs[4:])
    o_ref = rest.pop()
    x = x_ref[...]
    xn = _rms(x, nw_ref[...]).astype(BF16)
    acc = jnp.zeros_like(x)
    for c in range(D_FF // FFN_TF):
        lo = c * FFN_TF
        g = _dot(xn, win_ref[:, lo:lo + FFN_TF])
        u = _dot(xn, win_ref[:, D_FF + lo:D_FF + lo + FFN_TF])
        a = (g * jax.nn.sigmoid(g) * u).astype(BF16)
        acc = acc + _dot(a, wout_ref[lo:lo + FFN_TF, :])
    h = x + 0.5 * acc
    if ple:
        p_ref, gnw_ref, wgate_ref, wproj_ref = rest[:4]
        rest = rest[4:]
        hn = _rms(h, gnw_ref[...]).astype(BF16)
        gate = jax.nn.sigmoid(_dot(hn, wgate_ref[...]))
        h = h + gate * _dot(p_ref[...].astype(BF16), wproj_ref[...])
    if final:
        h = _rms(h, rest[0][...])
    o_ref[...] = h


def _ffn(h, nw, w_in, w_out, ple=None, final_w=None):
    T = h.shape[0]
    row = lambda t: (t, 0)
    fixed = lambda t: (0, 0)
    in_specs = [
        pl.BlockSpec((FFN_TM, D_MODEL), row),
        _resident((1, D_MODEL), fixed),
        _resident((D_MODEL, 2 * D_FF), fixed),
        _resident((D_FF, D_MODEL), fixed),
    ]
    args = [h, nw, w_in, w_out]
    if ple is not None:
        p, gnw, wgate, wproj = ple
        in_specs += [
            pl.BlockSpec((FFN_TM, D_PLE), row),
            _resident((1, D_MODEL), fixed),
            _resident((D_MODEL, D_MODEL), fixed),
            _resident((D_PLE, D_MODEL), fixed),
        ]
        args += [p, gnw, wgate, wproj]
    if final_w is not None:
        in_specs.append(_resident((1, D_MODEL), fixed))
        args.append(final_w)
    return pl.pallas_call(
        functools.partial(_ffn_kernel, ple=ple is not None, final=final_w is not None),
        grid=(T // FFN_TM,),
        in_specs=in_specs,
        out_specs=pl.BlockSpec((FFN_TM, D_MODEL), row),
        out_shape=jax.ShapeDtypeStruct((T, D_MODEL), F32),
        compiler_params=_params("parallel"),
        name="ffn",
    )(*args)


MIX_TM = 512


def _mix_kernel(a_ref, w_ref, res_ref, o_ref):
    o_ref[...] = res_ref[...] + _dot(a_ref[...], w_ref[...])


def _mix(a, w, res):
    T, K = a.shape
    row = lambda t: (t, 0)
    return pl.pallas_call(
        _mix_kernel,
        grid=(T // MIX_TM,),
        in_specs=[
            pl.BlockSpec((MIX_TM, K), row),
            _resident((K, D_MODEL), lambda t: (0, 0)),
            pl.BlockSpec((MIX_TM, D_MODEL), row),
        ],
        out_specs=pl.BlockSpec((MIX_TM, D_MODEL), row),
        out_shape=jax.ShapeDtypeStruct((T, D_MODEL), F32),
        compiler_params=_params("parallel"),
        name="mix",
    )(a, w, res)


RETP_TM = 512
RET_QK = RET_HEADS * RET_DK
RET_V = RET_HEADS * RET_DV


def _ret_proj_kernel(x_ref, nw_ref, w_ref, cos_ref, sin_ref, q_ref, k_ref, v_ref, g_ref):
    xn = _rms(x_ref[...], nw_ref[...]).astype(BF16)
    cos = cos_ref[...]
    sin = sin_ref[...]
    half = RET_DK // 2

    def rotary(col, out_ref, scale):
        for h in range(RET_HEADS):
            y = _dot(xn, w_ref[:, col + h * RET_DK:col + (h + 1) * RET_DK])
            x1, x2 = y[:, :half], y[:, half:]
            lo = h * RET_DK
            out_ref[:, lo:lo + half] = ((x1 * cos - x2 * sin) * scale).astype(BF16)
            out_ref[:, lo + half:lo + RET_DK] = ((x1 * sin + x2 * cos) * scale).astype(BF16)

    rotary(0, q_ref, 1.0)
    rotary(RET_QK, k_ref, RET_DK ** -0.5)
    for h in range(RET_HEADS):
        lo = h * RET_DV
        v_ref[:, lo:lo + RET_DV] = _dot(xn, w_ref[:, 2 * RET_QK + lo:2 * RET_QK + lo + RET_DV]).astype(BF16)
        g_ref[:, lo:lo + RET_DV] = _dot(
            xn, w_ref[:, 2 * RET_QK + RET_V + lo:2 * RET_QK + RET_V + lo + RET_DV]).astype(BF16)


def _ret_proj(h, nw, w_in, cos, sin, seq):
    T = h.shape[0]
    row = lambda t: (t, 0)
    fixed = lambda t: (0, 0)
    pos = lambda t: (t % (seq // RETP_TM), 0)
    n_in = 2 * RET_QK + 2 * RET_V
    return pl.pallas_call(
        _ret_proj_kernel,
        grid=(T // RETP_TM,),
        in_specs=[
            pl.BlockSpec((RETP_TM, D_MODEL), row),
            _resident((1, D_MODEL), fixed),
            _resident((D_MODEL, n_in), fixed),
            pl.BlockSpec((RETP_TM, RET_DK // 2), pos),
            pl.BlockSpec((RETP_TM, RET_DK // 2), pos),
        ],
        out_specs=[
            pl.BlockSpec((RETP_TM, RET_QK), row),
            pl.BlockSpec((RETP_TM, RET_QK), row),
            pl.BlockSpec((RETP_TM, RET_V), row),
            pl.BlockSpec((RETP_TM, RET_V), row),
        ],
        out_shape=[
            jax.ShapeDtypeStruct((T, RET_QK), BF16),
            jax.ShapeDtypeStruct((T, RET_QK), BF16),
            jax.ShapeDtypeStruct((T, RET_V), BF16),
            jax.ShapeDtypeStruct((T, RET_V), BF16),
        ],
        compiler_params=_params("parallel"),
        name="ret_proj",
    )(h, nw, w_in, cos, sin)


def _ret_core_kernel(gc_ref, q_ref, k_ref, v_ref, g_ref, res_ref, decay_ref, zeta_ref, xi_ref,
                     gn_ref, wo_ref, o_ref, state_ref):
    @pl.when(pl.program_id(1) == 0)
    def _():
        state_ref[...] = jnp.zeros_like(state_ref)

    acc = res_ref[...]
    for h in range(RET_HEADS):
        qh = q_ref[:, h * RET_DK:(h + 1) * RET_DK]
        kh = k_ref[:, h * RET_DK:(h + 1) * RET_DK]
        vh = v_ref[:, h * RET_DV:(h + 1) * RET_DV]
        st = state_ref[h]
        s = _dot_nt(qh, kh) * decay_ref[h]
        inner = _dot(s.astype(BF16), vh)
        cross = _dot(qh, st.astype(BF16)) * xi_ref[h]
        o = inner + cross
        kz = (kh.astype(F32) * zeta_ref[h]).astype(BF16)
        state_ref[h] = gc_ref[h] * st + _dot_tn(kz, vh)
        on = _rms(o, gn_ref[h])
        gh = g_ref[:, h * RET_DV:(h + 1) * RET_DV].astype(F32)
        y = (gh * jax.nn.sigmoid(gh) * on).astype(BF16)
        acc = acc + _dot(y, wo_ref[h * RET_DV:(h + 1) * RET_DV, :])
    o_ref[...] = acc


def _ret_core(q, k, v, g, res, gamma_c, decay, zeta, xi, gn_w, w_out, batch, seq):
    T = q.shape[0]
    C = RET_CHUNK
    nc = seq // C
    row = lambda b, c: (b * nc + c, 0)
    fixed3 = lambda b, c: (0, 0, 0)
    return pl.pallas_call(
        _ret_core_kernel,
        grid=(batch, nc),
        in_specs=[
            pl.BlockSpec(memory_space=pltpu.SMEM),
            pl.BlockSpec((C, RET_QK), row),
            pl.BlockSpec((C, RET_QK), row),
            pl.BlockSpec((C, RET_V), row),
            pl.BlockSpec((C, RET_V), row),
            pl.BlockSpec((C, D_MODEL), row),
            _resident((RET_HEADS, C, C), fixed3),
            _resident((RET_HEADS, C, 1), fixed3),
            _resident((RET_HEADS, C, 1), fixed3),
            _resident((RET_HEADS, 1, RET_DV), fixed3),
            _resident((RET_V, D_MODEL), lambda b, c: (0, 0)),
        ],
        out_specs=pl.BlockSpec((C, D_MODEL), row),
        out_shape=jax.ShapeDtypeStruct((T, D_MODEL), F32),
        scratch_shapes=[pltpu.VMEM((RET_HEADS, RET_DK, RET_DV), F32)],
        compiler_params=_params("parallel", "arbitrary"),
        name="ret_core",
    )(gamma_c, q, k, v, g, res, decay, zeta, xi, gn_w, w_out)


def _retention_tables(seq):
    H, C = RET_HEADS, RET_CHUNK
    half = RET_DK // 2
    pos = jnp.arange(seq, dtype=F32)
    inv_freq = ROPE_BASE ** (-jnp.arange(half, dtype=F32) / half)
    ang = pos[:, None] * inv_freq[None, :]
    log_gamma = jnp.log1p(-jnp.exp2(-5.0 - jnp.arange(H, dtype=F32)))
    idx = jnp.arange(C, dtype=F32)
    diff = idx[:, None] - idx[None, :]
    decay = jnp.where(diff[None] >= 0,
                      jnp.exp(log_gamma[:, None, None] * jnp.maximum(diff, 0.0)[None]), 0.0)
    zeta = jnp.exp(log_gamma[:, None] * (C - 1 - idx)[None, :])[:, :, None]
    xi = jnp.exp(log_gamma[:, None] * (idx + 1)[None, :])[:, :, None]
    gamma_c = jnp.exp(log_gamma * C)
    return jnp.cos(ang), jnp.sin(ang), gamma_c, decay, zeta, xi


FOXP_TM = 512
FOX_TQ = 512
FOX_SUB = 256
FOX_BK = 512
FOX_QK = FOX_HEADS * LANES


def _cumsum_rows(x):
    n = x.shape[0]
    row = lax.broadcasted_iota(jnp.int32, x.shape, 0)
    shift = 1
    while shift < n:
        x = x + jnp.where(row >= shift, pltpu.roll(x, shift, axis=0), 0.0)
        shift *= 2
    return x


def _split3(c):
    hi = c.astype(BF16).astype(F32)
    r = c - hi
    mid = r.astype(BF16).astype(F32)
    lo = (r - mid).astype(BF16).astype(F32)
    return hi, mid, lo


def _fox_proj_kernel(x_ref, nw_ref, w_ref, wf_ref, bf_ref, place_ref, ones_ref,
                     q_ref, k_ref, v_ref, carry_ref):
    @pl.when(pl.program_id(1) == 0)
    def _():
        carry_ref[...] = jnp.zeros_like(carry_ref)

    xn = _rms(x_ref[0], nw_ref[...]).astype(BF16)
    tm = xn.shape[0]

    z = _dot(xn, wf_ref[...]) + bf_ref[...]
    log_f = jnp.minimum(z, 0.0) - jnp.log(1.0 + jnp.exp(-jnp.abs(z)))
    c = _cumsum_rows(log_f) + carry_ref[...]
    carry_ref[...] = c[tm - 1:tm, :]

    hi, mid, lo = _split3(c)
    lane = lax.broadcasted_iota(jnp.int32, c.shape, 1)
    parts = jnp.where(lane < FOX_HEADS, hi,
                      jnp.where(lane < 2 * FOX_HEADS, pltpu.roll(mid, FOX_HEADS, axis=1),
                                pltpu.roll(lo, 2 * FOX_HEADS, axis=1))).astype(BF16)

    dh = FOX_DH
    nqk = FOX_HEADS * dh
    lane_t = lax.broadcasted_iota(jnp.int32, (tm, LANES), 1)
    v_aug = jnp.where(lane_t == dh, 1.0, 0.0)
    for which, out_ref, scale in ((0, q_ref, dh ** -0.5), (1, k_ref, 1.0), (2, v_ref, 1.0)):
        y_all = _dot(xn, w_ref[:, which * nqk:(which + 1) * nqk]) * scale
        for pair in range(FOX_HEADS // 2):
            y = y_all[:, pair * LANES:(pair + 1) * LANES]
            if which < 2:
                col = which * FOX_QK + 2 * pair * LANES
                aug2 = (_dot(parts, place_ref[:, col:col + 2 * LANES])
                        + ones_ref[:, col:col + 2 * LANES])
            for hh in range(2):
                aug = aug2[:, hh * LANES:(hh + 1) * LANES] if which < 2 else v_aug
                yh = y if hh == 0 else pltpu.roll(y, dh, axis=1)
                out_ref[0, 2 * pair + hh] = jnp.where(lane_t < dh, yh, aug).astype(BF16)


def _fox_tables():
    H = FOX_HEADS
    rows = jnp.arange(LANES)[:, None]
    cols = jnp.arange(2 * FOX_QK)[None, :]
    which = cols // FOX_QK
    head = (cols % FOX_QK) // LANES
    lane = cols % LANES
    first = jnp.where(which == 0, AUG_C, AUG_ONE)
    part = lane - first
    hit = (part >= 0) & (part < 3) & (rows == part * H + head)
    place = jnp.where(hit, jnp.where(which == 0, 1.0, -1.0), 0.0).astype(BF16)
    one_first = jnp.where(which == 0, AUG_ONE, AUG_C)
    ones = (((lane - one_first) >= 0) & ((lane - one_first) < 3)).astype(F32)
    return place, ones


def _fox_proj(h3, nw, w_qkv, w_f, b_f, place, ones):
    B, S, _ = h3.shape
    H = FOX_HEADS
    fixed = lambda b, t: (0, 0)
    return pl.pallas_call(
        _fox_proj_kernel,
        grid=(B, S // FOXP_TM),
        in_specs=[
            pl.BlockSpec((1, FOXP_TM, D_MODEL), lambda b, t: (b, t, 0)),
            _resident((1, D_MODEL), fixed),
            _resident((D_MODEL, 3 * D_MODEL), fixed),
            _resident((D_MODEL, LANES), fixed),
            _resident((1, LANES), fixed),
            _resident((LANES, 2 * FOX_QK), fixed),
            _resident((1, 2 * FOX_QK), fixed),
        ],
        out_specs=[
            pl.BlockSpec((1, H, FOXP_TM, LANES), lambda b, t: (b, 0, t, 0)),
            pl.BlockSpec((1, H, FOXP_TM, LANES), lambda b, t: (b, 0, t, 0)),
            pl.BlockSpec((1, H, FOXP_TM, LANES), lambda b, t: (b, 0, t, 0)),
        ],
        out_shape=[jax.ShapeDtypeStruct((B, H, S, LANES), BF16)] * 3,
        scratch_shapes=[pltpu.VMEM((1, LANES), F32)],
        compiler_params=_params("parallel", "arbitrary"),
        name="fox_proj",
    )(h3, nw, w_qkv, w_f, b_f, place, ones)


def _fox_attn_kernel(q_ref, k_ref, v_ref, o_ref):
    i = pl.program_id(2)
    tq, sub, bk = FOX_TQ, FOX_SUB, FOX_BK
    chains = [(hh, r) for hh in range(2) for r in range(tq // sub)]
    qs = [q_ref[0, hh, r * sub:(r + 1) * sub, :] for hh, r in chains]

    def scores(hh, q, j):
        return _dot_nt(q, k_ref[0, hh, pl.ds(pl.multiple_of(j * bk, bk), bk), :])

    def update(carry, s, vb):
        m, acc = carry
        m_new = jnp.maximum(m, jnp.max(s, axis=-1, keepdims=True))
        p = jnp.exp(s - m_new)
        acc = jnp.exp(m - m_new) * acc + _dot(p.astype(BF16), vb)
        return m_new, acc

    def body(j, carries):
        start = pl.multiple_of(j * bk, bk)
        return tuple(
            (scores(hh, q, j + 1),) + update(c[1:], c[0], v_ref[0, hh, pl.ds(start, bk), :])
            for (hh, _), q, c in zip(chains, qs, carries))

    init = tuple((scores(hh, q, 0), jnp.full((sub, 1), -jnp.inf, F32),
                  jnp.zeros((sub, LANES), F32)) for (hh, _), q in zip(chains, qs))
    carries = lax.fori_loop(0, i, body, init)

    base = pl.multiple_of(i * tq, tq)
    outs = {}
    for (hh, r), c in zip(chains, carries):
        row = lax.broadcasted_iota(jnp.int32, (sub, bk), 0) + r * sub
        col = lax.broadcasted_iota(jnp.int32, (sub, bk), 1)
        s = jnp.where(col <= row, c[0], -jnp.inf)
        _, acc = update(c[1:], s, v_ref[0, hh, pl.ds(base, bk), :])
        outs[hh, r] = acc / acc[:, FOX_DH:FOX_DH + 1]
    lane = lax.broadcasted_iota(jnp.int32, (sub, LANES), 1)
    for r in range(tq // sub):
        pair = jnp.where(lane < FOX_DH, outs[0, r], pltpu.roll(outs[1, r], FOX_DH, axis=1))
        o_ref[0, r * sub:(r + 1) * sub, :] = pair.astype(BF16)


def _fox_attn(q, k, v):
    B, H, S, _ = q.shape
    assert FOX_TQ == FOX_BK
    kv_spec = pl.BlockSpec((1, 2, S, LANES), lambda b, hp, i: (b, hp, 0, 0))
    return pl.pallas_call(
        _fox_attn_kernel,
        grid=(B, H // 2, S // FOX_TQ),
        in_specs=[
            pl.BlockSpec((1, 2, FOX_TQ, LANES), lambda b, hp, i: (b, hp, i, 0)),
            kv_spec,
            kv_spec,
        ],
        out_specs=pl.BlockSpec((1, FOX_TQ, LANES), lambda b, hp, i: (b, i, hp)),
        out_shape=jax.ShapeDtypeStruct((B, S, D_MODEL), BF16),
        compiler_params=_params("parallel", "parallel", "arbitrary"),
        name="fox_attn",
    )(q, k, v)


def kernel(x, p, norm_w, ffn_w_in, ffn_w_out, ret_w_in, ret_gn_w, ret_w_out, fox_w_in, fox_b_f,
           fox_w_out, ple_w_proj, ple_w_gate, final_norm_w):
    B, S, D = x.shape
    T = B * S
    depth = norm_w.shape[0]
    ffn_w_in = ffn_w_in.astype(BF16)
    ffn_w_out = ffn_w_out.astype(BF16)
    ret_w_in = ret_w_in.astype(BF16)
    ret_w_out = ret_w_out.astype(BF16)
    fox_w_in_b = fox_w_in.astype(BF16)
    fox_w_out = fox_w_out.astype(BF16)
    ple_w_proj = ple_w_proj.astype(BF16)
    ple_w_gate = ple_w_gate.astype(BF16)
    norm_w = norm_w.reshape(depth, 4, 1, D)
    final_w = final_norm_w.reshape(1, D)
    p = p.reshape(depth, T, D_PLE)

    cos, sin, gamma_c, decay, zeta, xi = _retention_tables(S)
    place, ones = _fox_tables()

    h = x.reshape(T, D)
    for i in range(depth):
        nw = norm_w[i]
        j = i // 2
        h = _ffn(h, nw[0], ffn_w_in[i, 0], ffn_w_out[i, 0])
        if i % 2 == 0:
            q, k, v, g = _ret_proj(h, nw[1], ret_w_in[j], cos, sin, S)
            h = _ret_core(q, k, v, g, h, gamma_c, decay, zeta, xi,
                          ret_gn_w[j].reshape(RET_HEADS, 1, RET_DV), ret_w_out[j], B, S)
        else:
            nqkv = 3 * D
            w_f = jnp.pad(fox_w_in_b[j][:, nqkv:], ((0, 0), (0, LANES - FOX_HEADS)))
            b_f = jnp.pad(fox_b_f[j], (0, LANES - FOX_HEADS)).reshape(1, LANES)
            qa, ka, v = _fox_proj(h.reshape(B, S, D), nw[1], fox_w_in_b[j][:, :nqkv], w_f, b_f,
                                  place, ones)
            o = _fox_attn(qa, ka, v)
            h = _mix(o.reshape(T, D), fox_w_out[j], h)
        h = _ffn(h, nw[2], ffn_w_in[i, 1], ffn_w_out[i, 1],
                 ple=(p[i], nw[3], ple_w_gate[i], ple_w_proj[i]),
                 final_w=final_w if i == depth - 1 else None)
    return h.reshape(B, S, D)
```

```python
import functools

import jax
import jax.numpy as jnp
from jax import lax
from jax.experimental import pallas as pl
from jax.experimental.pallas import tpu as pltpu

D_MODEL = 1024
D_PLE = 256
D_FF = 2816
EPS = 1e-6
RET_DK = 256
RET_HEADS = D_MODEL // RET_DK
RET_DV = 2 * RET_DK
RET_CHUNK = 128
ROPE_BASE = 10000.0
FOX_DH = 64
FOX_HEADS = D_MODEL // FOX_DH

LANES = 128
VMEM_LIMIT = 56 * 1024 * 1024

F32 = jnp.float32
BF16 = jnp.bfloat16

AUG_ONE = FOX_DH
AUG_C = FOX_DH + 3


def _rms(x, w):
    return x * lax.rsqrt(jnp.mean(x * x, axis=-1, keepdims=True) + EPS) * w


def _dot(a, b):
    return jnp.dot(a, b, preferred_element_type=F32)


def _dot_nt(a, b):
    return lax.dot_general(a, b, (((1,), (1,)), ((), ())), preferred_element_type=F32)


def _dot_tn(a, b):
    return lax.dot_general(a, b, (((0,), (0,)), ((), ())), preferred_element_type=F32)


def _resident(shape, index_map):
    return pl.BlockSpec(shape, index_map, pipeline_mode=pl.Buffered(1))


def _params(*sem):
    return pltpu.CompilerParams(dimension_semantics=sem, vmem_limit_bytes=VMEM_LIMIT)


FFN_TM = 512
FFN_TF = 256


def _ffn_kernel(*refs, mix, ple, final):
    refs = list(refs)
    o_ref = refs.pop()
    x_ref, nw_ref, win_ref, wout_ref = refs[:4]
    rest = refs[4:]
    x = x_ref[...]
    if mix:
        a_ref, wmix_ref = rest[:2]
        rest = rest[2:]
        x = x + _dot(a_ref[...], wmix_ref[...])
    xn = _rms(x, nw_ref[...]).astype(BF16)
    acc = jnp.zeros_like(x)
    for c in range(D_FF // FFN_TF):
        lo = c * FFN_TF
        g = _dot(xn, win_ref[:, lo:lo + FFN_TF])
        u = _dot(xn, win_ref[:, D_FF + lo:D_FF + lo + FFN_TF])
        a = (g * jax.nn.sigmoid(g) * u).astype(BF16)
        acc = acc + _dot(a, wout_ref[lo:lo + FFN_TF, :])
    h = x + 0.5 * acc
    if ple:
        p_ref, gnw_ref, wgate_ref, wproj_ref = rest[:4]
        rest = rest[4:]
        hn = _rms(h, gnw_ref[...]).astype(BF16)
        gate = jax.nn.sigmoid(_dot(hn, wgate_ref[...]))
        h = h + gate * _dot(p_ref[...].astype(BF16), wproj_ref[...])
    if final:
        h = _rms(h, rest[0][...])
    o_ref[...] = h


def _ffn(h, nw, w_in, w_out, mix=None, ple=None, final_w=None):
    T = h.shape[0]
    row = lambda t: (t, 0)
    fixed = lambda t: (0, 0)
    in_specs = [
        pl.BlockSpec((FFN_TM, D_MODEL), row),
        _resident((1, D_MODEL), fixed),
        _resident((D_MODEL, 2 * D_FF), fixed),
        _resident((D_FF, D_MODEL), fixed),
    ]
    args = [h, nw, w_in, w_out]
    if mix is not None:
        a, wmix = mix
        in_specs += [pl.BlockSpec((FFN_TM, a.shape[1]), row), _resident(wmix.shape, fixed)]
        args += [a, wmix]
    if ple is not None:
        p, gnw, wgate, wproj = ple
        in_specs += [
            pl.BlockSpec((FFN_TM, D_PLE), row),
            _resident((1, D_MODEL), fixed),
            _resident((D_MODEL, D_MODEL), fixed),
            _resident((D_PLE, D_MODEL), fixed),
        ]
        args += [p, gnw, wgate, wproj]
    if final_w is not None:
        in_specs.append(_resident((1, D_MODEL), fixed))
        args.append(final_w)
    return pl.pallas_call(
        functools.partial(_ffn_kernel, mix=mix is not None, ple=ple is not None,
                          final=final_w is not None),
        grid=(T // FFN_TM,),
        in_specs=in_specs,
        out_specs=pl.BlockSpec((FFN_TM, D_MODEL), row),
        out_shape=jax.ShapeDtypeStruct((T, D_MODEL), F32),
        compiler_params=_params("parallel"),
        name="ffn",
    )(*args)


RETP_TM = 512
RET_QK = RET_HEADS * RET_DK
RET_V = RET_HEADS * RET_DV


def _ret_proj_kernel(x_ref, nw_ref, w_ref, cos_ref, sin_ref, q_ref, k_ref, v_ref, g_ref):
    xn = _rms(x_ref[...], nw_ref[...]).astype(BF16)
    cos = cos_ref[...]
    sin = sin_ref[...]
    half = RET_DK // 2

    def rotary(col, out_ref, scale):
        for h in range(RET_HEADS):
            y = _dot(xn, w_ref[:, col + h * RET_DK:col + (h + 1) * RET_DK])
            x1, x2 = y[:, :half], y[:, half:]
            lo = h * RET_DK
            out_ref[:, lo:lo + half] = ((x1 * cos - x2 * sin) * scale).astype(BF16)
            out_ref[:, lo + half:lo + RET_DK] = ((x1 * sin + x2 * cos) * scale).astype(BF16)

    rotary(0, q_ref, 1.0)
    rotary(RET_QK, k_ref, RET_DK ** -0.5)
    for h in range(RET_HEADS):
        lo = h * RET_DV
        v_ref[:, lo:lo + RET_DV] = _dot(xn, w_ref[:, 2 * RET_QK + lo:2 * RET_QK + lo + RET_DV]).astype(BF16)
        g_ref[:, lo:lo + RET_DV] = _dot(
            xn, w_ref[:, 2 * RET_QK + RET_V + lo:2 * RET_QK + RET_V + lo + RET_DV]).astype(BF16)


def _ret_proj(h, nw, w_in, cos, sin, seq):
    T = h.shape[0]
    row = lambda t: (t, 0)
    fixed = lambda t: (0, 0)
    pos = lambda t: (t % (seq // RETP_TM), 0)
    n_in = 2 * RET_QK + 2 * RET_V
    return pl.pallas_call(
        _ret_proj_kernel,
        grid=(T // RETP_TM,),
        in_specs=[
            pl.BlockSpec((RETP_TM, D_MODEL), row),
            _resident((1, D_MODEL), fixed),
            _resident((D_MODEL, n_in), fixed),
            pl.BlockSpec((RETP_TM, RET_DK // 2), pos),
            pl.BlockSpec((RETP_TM, RET_DK // 2), pos),
        ],
        out_specs=[
            pl.BlockSpec((RETP_TM, RET_QK), row),
            pl.BlockSpec((RETP_TM, RET_QK), row),
            pl.BlockSpec((RETP_TM, RET_V), row),
            pl.BlockSpec((RETP_TM, RET_V), row),
        ],
        out_shape=[
            jax.ShapeDtypeStruct((T, RET_QK), BF16),
            jax.ShapeDtypeStruct((T, RET_QK), BF16),
            jax.ShapeDtypeStruct((T, RET_V), BF16),
            jax.ShapeDtypeStruct((T, RET_V), BF16),
        ],
        compiler_params=_params("parallel"),
        name="ret_proj",
    )(h, nw, w_in, cos, sin)


def _ret_core_kernel(gc_ref, q_ref, k_ref, v_ref, g_ref, res_ref, decay_ref, zeta_ref, xi_ref,
                     gn_ref, wo_ref, o_ref, state_ref):
    @pl.when(pl.program_id(1) == 0)
    def _():
        state_ref[...] = jnp.zeros_like(state_ref)

    acc = res_ref[...]
    for h in range(RET_HEADS):
        qh = q_ref[:, h * RET_DK:(h + 1) * RET_DK]
        kh = k_ref[:, h * RET_DK:(h + 1) * RET_DK]
        vh = v_ref[:, h * RET_DV:(h + 1) * RET_DV]
        st = state_ref[h]
        s = _dot_nt(qh, kh) * decay_ref[h]
        inner = _dot(s.astype(BF16), vh)
        cross = _dot(qh, st.astype(BF16)) * xi_ref[h]
        o = inner + cross
        kz = (kh.astype(F32) * zeta_ref[h]).astype(BF16)
        state_ref[h] = gc_ref[h] * st + _dot_tn(kz, vh)
        on = _rms(o, gn_ref[h])
        gh = g_ref[:, h * RET_DV:(h + 1) * RET_DV].astype(F32)
        y = (gh * jax.nn.sigmoid(gh) * on).astype(BF16)
        acc = acc + _dot(y, wo_ref[h * RET_DV:(h + 1) * RET_DV, :])
    o_ref[...] = acc


def _ret_core(q, k, v, g, res, gamma_c, decay, zeta, xi, gn_w, w_out, batch, seq):
    T = q.shape[0]
    C = RET_CHUNK
    nc = seq // C
    row = lambda b, c: (b * nc + c, 0)
    fixed3 = lambda b, c: (0, 0, 0)
    return pl.pallas_call(
        _ret_core_kernel,
        grid=(batch, nc),
        in_specs=[
            pl.BlockSpec(memory_space=pltpu.SMEM),
            pl.BlockSpec((C, RET_QK), row),
            pl.BlockSpec((C, RET_QK), row),
            pl.BlockSpec((C, RET_V), row),
            pl.BlockSpec((C, RET_V), row),
            pl.BlockSpec((C, D_MODEL), row),
            _resident((RET_HEADS, C, C), fixed3),
            _resident((RET_HEADS, C, 1), fixed3),
            _resident((RET_HEADS, C, 1), fixed3),
            _resident((RET_HEADS, 1, RET_DV), fixed3),
            _resident((RET_V, D_MODEL), lambda b, c: (0, 0)),
        ],
        out_specs=pl.BlockSpec((C, D_MODEL), row),
        out_shape=jax.ShapeDtypeStruct((T, D_MODEL), F32),
        scratch_shapes=[pltpu.VMEM((RET_HEADS, RET_DK, RET_DV), F32)],
        compiler_params=_params("parallel", "arbitrary"),
        name="ret_core",
    )(gamma_c, q, k, v, g, res, decay, zeta, xi, gn_w, w_out)


def _retention_tables(seq):
    H, C = RET_HEADS, RET_CHUNK
    half = RET_DK // 2
    pos = jnp.arange(seq, dtype=F32)
    inv_freq = ROPE_BASE ** (-jnp.arange(half, dtype=F32) / half)
    ang = pos[:, None] * inv_freq[None, :]
    log_gamma = jnp.log1p(-jnp.exp2(-5.0 - jnp.arange(H, dtype=F32)))
    idx = jnp.arange(C, dtype=F32)
    diff = idx[:, None] - idx[None, :]
    decay = jnp.where(diff[None] >= 0,
                      jnp.exp(log_gamma[:, None, None] * jnp.maximum(diff, 0.0)[None]), 0.0)
    zeta = jnp.exp(log_gamma[:, None] * (C - 1 - idx)[None, :])[:, :, None]
    xi = jnp.exp(log_gamma[:, None] * (idx + 1)[None, :])[:, :, None]
    gamma_c = jnp.exp(log_gamma * C)
    return jnp.cos(ang), jnp.sin(ang), gamma_c, decay, zeta, xi


FOXP_TM = 512
FOX_TQ = 512
FOX_SUB = 256
FOX_BK = 256
FOX_QK = FOX_HEADS * LANES
FOX_VT = FOX_DH + 16
LOG2E = 1.4426950408889634


def _cumsum_rows(x):
    n = x.shape[0]
    row = lax.broadcasted_iota(jnp.int32, x.shape, 0)
    shift = 1
    while shift < n:
        x = x + jnp.where(row >= shift, pltpu.roll(x, shift, axis=0), 0.0)
        shift *= 2
    return x


def _split3(c):
    hi = c.astype(BF16).astype(F32)
    r = c - hi
    mid = r.astype(BF16).astype(F32)
    lo = (r - mid).astype(BF16).astype(F32)
    return hi, mid, lo


def _fox_proj_kernel(x_ref, nw_ref, w_ref, wf_ref, bf_ref, place_ref, ones_ref,
                     q_ref, k_ref, vt_ref, carry_ref):
    @pl.when(pl.program_id(1) == 0)
    def _():
        carry_ref[...] = jnp.zeros_like(carry_ref)

    xn = _rms(x_ref[0], nw_ref[...]).astype(BF16)
    tm = xn.shape[0]

    z = _dot(xn, wf_ref[...]) + bf_ref[...]
    log_f = jnp.minimum(z, 0.0) - jnp.log(1.0 + jnp.exp(-jnp.abs(z)))
    c = _cumsum_rows(log_f) + carry_ref[...]
    carry_ref[...] = c[tm - 1:tm, :]

    hi, mid, lo = _split3(c * LOG2E)
    lane = lax.broadcasted_iota(jnp.int32, c.shape, 1)
    parts = jnp.where(lane < FOX_HEADS, hi,
                      jnp.where(lane < 2 * FOX_HEADS, pltpu.roll(mid, FOX_HEADS, axis=1),
                                pltpu.roll(lo, 2 * FOX_HEADS, axis=1))).astype(BF16)

    dh = FOX_DH
    nqk = FOX_HEADS * dh
    lane_t = lax.broadcasted_iota(jnp.int32, (tm, LANES), 1)
    for which, out_ref, scale in ((0, q_ref, dh ** -0.5 * LOG2E), (1, k_ref, 1.0)):
        y_all = _dot(xn, w_ref[:, which * nqk:(which + 1) * nqk]) * scale
        for pair in range(FOX_HEADS // 2):
            y = y_all[:, pair * LANES:(pair + 1) * LANES]
            col = which * FOX_QK + 2 * pair * LANES
            aug2 = _dot(parts, place_ref[:, col:col + 2 * LANES]) + ones_ref[:, col:col + 2 * LANES]
            for hh in range(2):
                yh = y if hh == 0 else pltpu.roll(y, dh, axis=1)
                out_ref[0, 2 * pair + hh] = jnp.where(
                    lane_t < dh, yh, aug2[:, hh * LANES:(hh + 1) * LANES]).astype(BF16)

    sub_t = lax.broadcasted_iota(jnp.int32, (FOX_VT - dh, tm), 0)
    v_aug = jnp.where(sub_t == 0, 1.0, 0.0).astype(BF16)
    y_all = _dot(xn, w_ref[:, 2 * nqk:3 * nqk])
    for pair in range(FOX_HEADS // 2):
        yt = y_all[:, pair * LANES:(pair + 1) * LANES].T
        for hh in range(2):
            vt_ref[0, 2 * pair + hh, 0:dh, :] = yt[hh * dh:(hh + 1) * dh, :].astype(BF16)
            vt_ref[0, 2 * pair + hh, dh:FOX_VT, :] = v_aug


def _fox_tables():
    H = FOX_HEADS
    rows = jnp.arange(LANES)[:, None]
    cols = jnp.arange(2 * FOX_QK)[None, :]
    which = cols // FOX_QK
    head = (cols % FOX_QK) // LANES
    lane = cols % LANES
    first = jnp.where(which == 0, AUG_C, AUG_ONE)
    part = lane - first
    hit = (part >= 0) & (part < 3) & (rows == part * H + head)
    place = jnp.where(hit, jnp.where(which == 0, 1.0, -1.0), 0.0).astype(BF16)
    one_first = jnp.where(which == 0, AUG_ONE, AUG_C)
    ones = (((lane - one_first) >= 0) & ((lane - one_first) < 3)).astype(F32)
    return place, ones


def _fox_proj(h3, nw, w_qkv, w_f, b_f, place, ones):
    B, S, _ = h3.shape
    H = FOX_HEADS
    fixed = lambda b, t: (0, 0)
    return pl.pallas_call(
        _fox_proj_kernel,
        grid=(B, S // FOXP_TM),
        in_specs=[
            pl.BlockSpec((1, FOXP_TM, D_MODEL), lambda b, t: (b, t, 0)),
            _resident((1, D_MODEL), fixed),
            _resident((D_MODEL, 3 * D_MODEL), fixed),
            _resident((D_MODEL, LANES), fixed),
            _resident((1, LANES), fixed),
            _resident((LANES, 2 * FOX_QK), fixed),
            _resident((1, 2 * FOX_QK), fixed),
        ],
        out_specs=[
            pl.BlockSpec((1, H, FOXP_TM, LANES), lambda b, t: (b, 0, t, 0)),
            pl.BlockSpec((1, H, FOXP_TM, LANES), lambda b, t: (b, 0, t, 0)),
            pl.BlockSpec((1, H, FOX_VT, FOXP_TM), lambda b, t: (b, 0, 0, t)),
        ],
        out_shape=[jax.ShapeDtypeStruct((B, H, S, LANES), BF16)] * 2
        + [jax.ShapeDtypeStruct((B, H, FOX_VT, S), BF16)],
        scratch_shapes=[pltpu.VMEM((1, LANES), F32)],
        compiler_params=_params("parallel", "arbitrary"),
        name="fox_proj",
    )(h3, nw, w_qkv, w_f, b_f, place, ones)


def _fox_attn_kernel(q_ref, k_ref, vt_ref, o_ref, s_ref, m_ref, acc_ref):
    i = pl.program_id(2)
    sub, bk = FOX_SUB, FOX_BK
    chains = [(hh, r) for hh in range(2) for r in range(2)]

    def key_tile(j):
        return pl.ds(pl.multiple_of(j * bk, bk), bk)

    def issue(slot, c, j):
        hh, r = chains[c]
        s_ref[slot, c] = _dot_nt(k_ref[0, hh, key_tile(j), :], q_ref[0, hh, r * sub:(r + 1) * sub, :])

    def consume(slot, c, j, masked):
        hh, _ = chains[c]
        s_t = s_ref[slot, c]
        if masked:
            key = lax.broadcasted_iota(jnp.int32, (bk, sub), 0)
            query = lax.broadcasted_iota(jnp.int32, (bk, sub), 1)
            s_t = jnp.where(key <= query, s_t, -jnp.inf)
        m = m_ref[c]
        m_new = jnp.maximum(m, jnp.max(s_t, axis=0, keepdims=True))
        p_t = jnp.exp2(s_t - m_new).astype(BF16)
        acc_ref[c] = jnp.exp2(m - m_new) * acc_ref[c] + _dot(vt_ref[0, hh, :, key_tile(j)], p_t)
        m_ref[c] = m_new

    m_ref[...] = jnp.full_like(m_ref, -jnp.inf)
    acc_ref[...] = jnp.zeros_like(acc_ref)
    for c in range(4):
        issue(0, c, 0)

    @pl.loop(0, i)
    def _(u):
        for c in range(4):
            issue(1, c, 2 * u + 1)
            consume(0, c, 2 * u, False)
        for c in range(4):
            issue(0, c, 2 * u + 2)
            consume(1, c, 2 * u + 1, False)

    for c, (_, r) in enumerate(chains):
        if r == 1:
            issue(1, c, 2 * i + 1)
        consume(0, c, 2 * i, masked=(r == 0))
    for c, (_, r) in enumerate(chains):
        if r == 1:
            consume(1, c, 2 * i + 1, masked=True)

    for r in range(2):
        out_t = [acc_ref[2 * hh + r, 0:FOX_DH, :] / acc_ref[2 * hh + r, FOX_DH:FOX_DH + 1, :]
                 for hh in range(2)]
        o_ref[0, r * sub:(r + 1) * sub, :] = jnp.concatenate(out_t, axis=0).T.astype(BF16)


def _fox_attn(q, k, vt):
    B, H, S, _ = q.shape
    assert FOX_TQ == 2 * FOX_BK == 2 * FOX_SUB
    return pl.pallas_call(
        _fox_attn_kernel,
        grid=(B, H // 2, S // FOX_TQ),
        in_specs=[
            pl.BlockSpec((1, 2, FOX_TQ, LANES), lambda b, hp, i: (b, hp, i, 0)),
            pl.BlockSpec((1, 2, S, LANES), lambda b, hp, i: (b, hp, 0, 0)),
            pl.BlockSpec((1, 2, FOX_VT, S), lambda b, hp, i: (b, hp, 0, 0)),
        ],
        out_specs=pl.BlockSpec((1, FOX_TQ, LANES), lambda b, hp, i: (b, i, hp)),
        out_shape=jax.ShapeDtypeStruct((B, S, D_MODEL), BF16),
        scratch_shapes=[
            pltpu.VMEM((2, 4, FOX_BK, FOX_SUB), F32),
            pltpu.VMEM((4, 1, FOX_SUB), F32),
            pltpu.VMEM((4, FOX_VT, FOX_SUB), F32),
        ],
        compiler_params=_params("parallel", "parallel", "arbitrary"),
        name="fox_attn",
    )(q, k, vt)


def kernel(x, p, norm_w, ffn_w_in, ffn_w_out, ret_w_in, ret_gn_w, ret_w_out, fox_w_in, fox_b_f,
           fox_w_out, ple_w_proj, ple_w_gate, final_norm_w):
    B, S, D = x.shape
    T = B * S
    depth = norm_w.shape[0]
    ffn_w_in = ffn_w_in.astype(BF16)
    ffn_w_out = ffn_w_out.astype(BF16)
    ret_w_in = ret_w_in.astype(BF16)
    ret_w_out = ret_w_out.astype(BF16)
    fox_w_in_b = fox_w_in.astype(BF16)
    fox_w_out = fox_w_out.astype(BF16)
    ple_w_proj = ple_w_proj.astype(BF16)
    ple_w_gate = ple_w_gate.astype(BF16)
    norm_w = norm_w.reshape(depth, 4, 1, D)
    final_w = final_norm_w.reshape(1, D)
    p = p.reshape(depth, T, D_PLE)

    cos, sin, gamma_c, decay, zeta, xi = _retention_tables(S)
    place, ones = _fox_tables()

    h = x.reshape(T, D)
    for i in range(depth):
        nw = norm_w[i]
        j = i // 2
        h = _ffn(h, nw[0], ffn_w_in[i, 0], ffn_w_out[i, 0])
        mix = None
        if i % 2 == 0:
            q, k, v, g = _ret_proj(h, nw[1], ret_w_in[j], cos, sin, S)
            h = _ret_core(q, k, v, g, h, gamma_c, decay, zeta, xi,
                          ret_gn_w[j].reshape(RET_HEADS, 1, RET_DV), ret_w_out[j], B, S)
        else:
            nqkv = 3 * D
            w_f = jnp.pad(fox_w_in_b[j][:, nqkv:], ((0, 0), (0, LANES - FOX_HEADS)))
            b_f = jnp.pad(fox_b_f[j], (0, LANES - FOX_HEADS)).reshape(1, LANES)
            qa, ka, vt = _fox_proj(h.reshape(B, S, D), nw[1], fox_w_in_b[j][:, :nqkv], w_f, b_f,
                                   place, ones)
            mix = (_fox_attn(qa, ka, vt).reshape(T, D), fox_w_out[j])
        h = _ffn(h, nw[2], ffn_w_in[i, 1], ffn_w_out[i, 1], mix=mix,
                 ple=(p[i], nw[3], ple_w_gate[i], ple_w_proj[i]),
                 final_w=final_w if i == depth - 1 else None)
    return h.reshape(B, S, D)
```

```python
import functools

import jax
import jax.numpy as jnp
from jax import lax
from jax.experimental import pallas as pl
from jax.experimental.pallas import tpu as pltpu

D_MODEL = 1024
D_PLE = 256
D_FF = 2816
EPS = 1e-6
RET_DK = 256
RET_HEADS = D_MODEL // RET_DK
RET_DV = 2 * RET_DK
RET_CHUNK = 256
ROPE_BASE = 10000.0
FOX_DH = 64
FOX_HEADS = D_MODEL // FOX_DH

LANES = 128
VMEM_LIMIT = 56 * 1024 * 1024

F32 = jnp.float32
BF16 = jnp.bfloat16

AUG_ONE = FOX_DH
AUG_C = FOX_DH + 3


def _rms(x, w):
    return x * lax.rsqrt(jnp.mean(x * x, axis=-1, keepdims=True) + EPS) * w


def _dot(a, b):
    return jnp.dot(a, b, preferred_element_type=F32)


def _dot_nt(a, b):
    return lax.dot_general(a, b, (((1,), (1,)), ((), ())), preferred_element_type=F32)


def _dot_tn(a, b):
    return lax.dot_general(a, b, (((0,), (0,)), ((), ())), preferred_element_type=F32)


def _resident(shape, index_map):
    return pl.BlockSpec(shape, index_map, pipeline_mode=pl.Buffered(1))


def _layer(arr, *lead, cols=None):
    tail = arr.shape[len(lead):]
    if cols is not None:
        tail = tail[:-1] + (cols,)
    idx = tuple(lead) + (0,) * len(tail)
    return _resident((None,) * len(lead) + tail, lambda *_: idx)


def _params(*sem):
    return pltpu.CompilerParams(dimension_semantics=sem, vmem_limit_bytes=VMEM_LIMIT)


FFN_TM = 512
FFN_TF = 256


def _ffn_kernel(*refs, mix, ple, final):
    refs = list(refs)
    o_ref = refs.pop()
    x_ref, nw_ref, win_ref, wout_ref = refs[:4]
    rest = refs[4:]
    x = x_ref[...]
    if mix:
        a_ref, wmix_ref = rest[:2]
        rest = rest[2:]
        x = x + _dot(a_ref[...], wmix_ref[...])
    xn = _rms(x, nw_ref[...]).astype(BF16)
    acc = jnp.zeros_like(x)
    for c in range(D_FF // FFN_TF):
        lo = c * FFN_TF
        g = _dot(xn, win_ref[:, lo:lo + FFN_TF])
        u = _dot(xn, win_ref[:, D_FF + lo:D_FF + lo + FFN_TF])
        a = (g * jax.nn.sigmoid(g) * u).astype(BF16)
        acc = acc + _dot(a, wout_ref[lo:lo + FFN_TF, :])
    h = x + 0.5 * acc
    if ple:
        p_ref, gnw_ref, wgate_ref, wproj_ref = rest[:4]
        rest = rest[4:]
        hn = _rms(h, gnw_ref[...]).astype(BF16)
        gate = jax.nn.sigmoid(_dot(hn, wgate_ref[...]))
        h = h + gate * _dot(p_ref[...].astype(BF16), wproj_ref[...])
    if final:
        h = _rms(h, rest[0][...])
    o_ref[...] = h


def _ffn(h, norm_w, w_in, w_out, i, k, mix=None, ple=None, final_w=None):
    T = h.shape[0]
    row = lambda t: (t, 0)
    in_specs = [
        pl.BlockSpec((FFN_TM, D_MODEL), row),
        _layer(norm_w, i, 2 * k),
        _layer(w_in, i, k),
        _layer(w_out, i, k),
    ]
    args = [h, norm_w, w_in, w_out]
    if mix is not None:
        a, wmix, j = mix
        in_specs += [pl.BlockSpec((FFN_TM, a.shape[1]), row), _layer(wmix, j)]
        args += [a, wmix]
    if ple is not None:
        p, wgate, wproj = ple
        in_specs += [
            pl.BlockSpec((None, FFN_TM, D_PLE), lambda t: (i, t, 0)),
            _layer(norm_w, i, 3),
            _layer(wgate, i),
            _layer(wproj, i),
        ]
        args += [p, norm_w, wgate, wproj]
    if final_w is not None:
        in_specs.append(_resident((1, D_MODEL), lambda t: (0, 0)))
        args.append(final_w)
    return pl.pallas_call(
        functools.partial(_ffn_kernel, mix=mix is not None, ple=ple is not None,
                          final=final_w is not None),
        grid=(T // FFN_TM,),
        in_specs=in_specs,
        out_specs=pl.BlockSpec((FFN_TM, D_MODEL), row),
        out_shape=jax.ShapeDtypeStruct((T, D_MODEL), F32),
        compiler_params=_params("parallel"),
        name="ffn",
    )(*args)


RETP_TM = 512
RET_QK = RET_HEADS * RET_DK
RET_V = RET_HEADS * RET_DV


def _ret_proj_kernel(x_ref, nw_ref, w_ref, cos_ref, sin_ref, q_ref, k_ref, v_ref, g_ref):
    xn = _rms(x_ref[...], nw_ref[...]).astype(BF16)
    cos = cos_ref[...]
    sin = sin_ref[...]
    half = RET_DK // 2

    def rotary(col, out_ref, scale):
        for h in range(RET_HEADS):
            y = _dot(xn, w_ref[:, col + h * RET_DK:col + (h + 1) * RET_DK])
            x1, x2 = y[:, :half], y[:, half:]
            lo = h * RET_DK
            out_ref[:, lo:lo + half] = ((x1 * cos - x2 * sin) * scale).astype(BF16)
            out_ref[:, lo + half:lo + RET_DK] = ((x1 * sin + x2 * cos) * scale).astype(BF16)

    rotary(0, q_ref, 1.0)
    rotary(RET_QK, k_ref, RET_DK ** -0.5)
    for h in range(RET_HEADS):
        lo = h * RET_DV
        v_ref[:, lo:lo + RET_DV] = _dot(xn, w_ref[:, 2 * RET_QK + lo:2 * RET_QK + lo + RET_DV]).astype(BF16)
        g_ref[:, lo:lo + RET_DV] = _dot(
            xn, w_ref[:, 2 * RET_QK + RET_V + lo:2 * RET_QK + RET_V + lo + RET_DV]).astype(BF16)


def _ret_proj(h, norm_w, w_in, i, j, cos, sin, seq):
    T = h.shape[0]
    row = lambda t: (t, 0)
    pos = lambda t: (t % (seq // RETP_TM), 0)
    return pl.pallas_call(
        _ret_proj_kernel,
        grid=(T // RETP_TM,),
        in_specs=[
            pl.BlockSpec((RETP_TM, D_MODEL), row),
            _layer(norm_w, i, 1),
            _layer(w_in, j),
            pl.BlockSpec((RETP_TM, RET_DK // 2), pos),
            pl.BlockSpec((RETP_TM, RET_DK // 2), pos),
        ],
        out_specs=[
            pl.BlockSpec((RETP_TM, RET_QK), row),
            pl.BlockSpec((RETP_TM, RET_QK), row),
            pl.BlockSpec((RETP_TM, RET_V), row),
            pl.BlockSpec((RETP_TM, RET_V), row),
        ],
        out_shape=[
            jax.ShapeDtypeStruct((T, RET_QK), BF16),
            jax.ShapeDtypeStruct((T, RET_QK), BF16),
            jax.ShapeDtypeStruct((T, RET_V), BF16),
            jax.ShapeDtypeStruct((T, RET_V), BF16),
        ],
        compiler_params=_params("parallel"),
        name="ret_proj",
    )(h, norm_w, w_in, cos, sin)


def _ret_core_kernel(gc_ref, q_ref, k_ref, v_ref, g_ref, res_ref, decay_ref, zeta_ref, xi_ref,
                     gn_ref, wo_ref, o_ref, state_ref):
    @pl.when(pl.program_id(1) == 0)
    def _():
        state_ref[...] = jnp.zeros_like(state_ref)

    acc = res_ref[...]
    for h in range(RET_HEADS):
        qh = q_ref[:, h * RET_DK:(h + 1) * RET_DK]
        kh = k_ref[:, h * RET_DK:(h + 1) * RET_DK]
        vh = v_ref[:, h * RET_DV:(h + 1) * RET_DV]
        st = state_ref[h]
        s = _dot_nt(qh, kh) * decay_ref[h]
        inner = _dot(s.astype(BF16), vh)
        cross = _dot(qh, st.astype(BF16)) * xi_ref[h]
        o = inner + cross
        kz = (kh.astype(F32) * zeta_ref[h]).astype(BF16)
        state_ref[h] = gc_ref[h] * st + _dot_tn(kz, vh)
        on = _rms(o, gn_ref[h])
        gh = g_ref[:, h * RET_DV:(h + 1) * RET_DV].astype(F32)
        y = (gh * jax.nn.sigmoid(gh) * on).astype(BF16)
        acc = acc + _dot(y, wo_ref[h * RET_DV:(h + 1) * RET_DV, :])
    o_ref[...] = acc


def _ret_core(q, k, v, g, res, gamma_c, decay, zeta, xi, gn_w, w_out, j, batch, seq):
    T = q.shape[0]
    C = RET_CHUNK
    nc = seq // C
    row = lambda b, c: (b * nc + c, 0)
    fixed3 = lambda b, c: (0, 0, 0)
    return pl.pallas_call(
        _ret_core_kernel,
        grid=(batch, nc),
        in_specs=[
            pl.BlockSpec(memory_space=pltpu.SMEM),
            pl.BlockSpec((C, RET_QK), row),
            pl.BlockSpec((C, RET_QK), row),
            pl.BlockSpec((C, RET_V), row),
            pl.BlockSpec((C, RET_V), row),
            pl.BlockSpec((C, D_MODEL), row),
            _resident((RET_HEADS, C, C), fixed3),
            _resident((RET_HEADS, C, 1), fixed3),
            _resident((RET_HEADS, C, 1), fixed3),
            _layer(gn_w, j),
            _layer(w_out, j),
        ],
        out_specs=pl.BlockSpec((C, D_MODEL), row),
        out_shape=jax.ShapeDtypeStruct((T, D_MODEL), F32),
        scratch_shapes=[pltpu.VMEM((RET_HEADS, RET_DK, RET_DV), F32)],
        compiler_params=_params("parallel", "arbitrary"),
        name="ret_core",
    )(gamma_c, q, k, v, g, res, decay, zeta, xi, gn_w, w_out)


def _retention_tables(seq):
    H, C = RET_HEADS, RET_CHUNK
    half = RET_DK // 2
    pos = jnp.arange(seq, dtype=F32)
    inv_freq = ROPE_BASE ** (-jnp.arange(half, dtype=F32) / half)
    ang = pos[:, None] * inv_freq[None, :]
    log_gamma = jnp.log1p(-jnp.exp2(-5.0 - jnp.arange(H, dtype=F32)))
    idx = jnp.arange(C, dtype=F32)
    diff = idx[:, None] - idx[None, :]
    decay = jnp.where(diff[None] >= 0,
                      jnp.exp(log_gamma[:, None, None] * jnp.maximum(diff, 0.0)[None]), 0.0)
    zeta = jnp.exp(log_gamma[:, None] * (C - 1 - idx)[None, :])[:, :, None]
    xi = jnp.exp(log_gamma[:, None] * (idx + 1)[None, :])[:, :, None]
    gamma_c = jnp.exp(log_gamma * C)
    return jnp.cos(ang), jnp.sin(ang), gamma_c, decay, zeta, xi


FOXP_TM = 512
FOX_SUB = 256
FOX_NSUB = 4
FOX_TQ = FOX_NSUB * FOX_SUB
FOX_BK = 256
FOX_QK = FOX_HEADS * LANES
FOX_VT = FOX_DH + 16
LOG2E = 1.4426950408889634


def _cumsum_rows(x):
    n = x.shape[0]
    row = lax.broadcasted_iota(jnp.int32, x.shape, 0)
    shift = 1
    while shift < n:
        x = x + jnp.where(row >= shift, pltpu.roll(x, shift, axis=0), 0.0)
        shift *= 2
    return x


def _split3(c):
    hi = c.astype(BF16).astype(F32)
    r = c - hi
    mid = r.astype(BF16).astype(F32)
    lo = (r - mid).astype(BF16).astype(F32)
    return hi, mid, lo


def _fox_proj_kernel(x_ref, nw_ref, w_ref, wf_ref, bf_ref, place_ref, ones_ref,
                     q_ref, k_ref, vt_ref, carry_ref):
    @pl.when(pl.program_id(1) == 0)
    def _():
        carry_ref[...] = jnp.zeros_like(carry_ref)

    xn = _rms(x_ref[0], nw_ref[...]).astype(BF16)
    tm = xn.shape[0]

    z = _dot(xn, wf_ref[...]) + bf_ref[...]
    log_f = jnp.minimum(z, 0.0) - jnp.log(1.0 + jnp.exp(-jnp.abs(z)))
    c = _cumsum_rows(log_f) + carry_ref[...]
    carry_ref[...] = c[tm - 1:tm, :]

    hi, mid, lo = _split3(c * LOG2E)
    lane = lax.broadcasted_iota(jnp.int32, c.shape, 1)
    parts = jnp.where(lane < FOX_HEADS, hi,
                      jnp.where(lane < 2 * FOX_HEADS, pltpu.roll(mid, FOX_HEADS, axis=1),
                                pltpu.roll(lo, 2 * FOX_HEADS, axis=1))).astype(BF16)

    dh = FOX_DH
    nqk = FOX_HEADS * dh
    lane_t = lax.broadcasted_iota(jnp.int32, (tm, LANES), 1)
    for which, out_ref, scale in ((0, q_ref, dh ** -0.5 * LOG2E), (1, k_ref, 1.0)):
        y_all = _dot(xn, w_ref[:, which * nqk:(which + 1) * nqk]) * scale
        for pair in range(FOX_HEADS // 2):
            y = y_all[:, pair * LANES:(pair + 1) * LANES]
            col = which * FOX_QK + 2 * pair * LANES
            aug2 = _dot(parts, place_ref[:, col:col + 2 * LANES]) + ones_ref[:, col:col + 2 * LANES]
            for hh in range(2):
                yh = y if hh == 0 else pltpu.roll(y, dh, axis=1)
                out_ref[0, 2 * pair + hh] = jnp.where(
                    lane_t < dh, yh, aug2[:, hh * LANES:(hh + 1) * LANES]).astype(BF16)

    sub_t = lax.broadcasted_iota(jnp.int32, (FOX_VT - dh, tm), 0)
    v_aug = jnp.where(sub_t == 0, 1.0, 0.0).astype(BF16)
    y_all = _dot(xn, w_ref[:, 2 * nqk:3 * nqk])
    for pair in range(FOX_HEADS // 2):
        yt = y_all[:, pair * LANES:(pair + 1) * LANES].T
        for hh in range(2):
            vt_ref[0, 2 * pair + hh, 0:dh, :] = yt[hh * dh:(hh + 1) * dh, :].astype(BF16)
            vt_ref[0, 2 * pair + hh, dh:FOX_VT, :] = v_aug


def _fox_tables():
    H = FOX_HEADS
    rows = jnp.arange(LANES)[:, None]
    cols = jnp.arange(2 * FOX_QK)[None, :]
    which = cols // FOX_QK
    head = (cols % FOX_QK) // LANES
    lane = cols % LANES
    first = jnp.where(which == 0, AUG_C, AUG_ONE)
    part = lane - first
    hit = (part >= 0) & (part < 3) & (rows == part * H + head)
    place = jnp.where(hit, jnp.where(which == 0, 1.0, -1.0), 0.0).astype(BF16)
    one_first = jnp.where(which == 0, AUG_ONE, AUG_C)
    ones = (((lane - one_first) >= 0) & ((lane - one_first) < 3)).astype(F32)
    return place, ones


def _fox_proj(h3, norm_w, w_in, i, j, w_f, b_f, place, ones):
    B, S, _ = h3.shape
    H = FOX_HEADS
    fixed = lambda b, t: (0, 0)
    return pl.pallas_call(
        _fox_proj_kernel,
        grid=(B, S // FOXP_TM),
        in_specs=[
            pl.BlockSpec((1, FOXP_TM, D_MODEL), lambda b, t: (b, t, 0)),
            _layer(norm_w, i, 1),
            _layer(w_in, j, cols=3 * D_MODEL),
            _resident((D_MODEL, LANES), fixed),
            _resident((1, LANES), fixed),
            _resident((LANES, 2 * FOX_QK), fixed),
            _resident((1, 2 * FOX_QK), fixed),
        ],
        out_specs=[
            pl.BlockSpec((1, H, FOXP_TM, LANES), lambda b, t: (b, 0, t, 0)),
            pl.BlockSpec((1, H, FOXP_TM, LANES), lambda b, t: (b, 0, t, 0)),
            pl.BlockSpec((1, H, FOX_VT, FOXP_TM), lambda b, t: (b, 0, 0, t)),
        ],
        out_shape=[jax.ShapeDtypeStruct((B, H, S, LANES), BF16)] * 2
        + [jax.ShapeDtypeStruct((B, H, FOX_VT, S), BF16)],
        scratch_shapes=[pltpu.VMEM((1, LANES), F32)],
        compiler_params=_params("parallel", "arbitrary"),
        name="fox_proj",
    )(h3, norm_w, w_in, w_f, b_f, place, ones)


def _fox_attn_kernel(q_ref, k_ref, vt_ref, o_ref, s_ref, m_ref, acc_ref):
    i = pl.program_id(2)
    sub, bk, n_sub = FOX_SUB, FOX_BK, FOX_NSUB
    chains = [(hh, r) for hh in range(2) for r in range(n_sub)]

    def key_tile(j):
        return pl.ds(pl.multiple_of(j * bk, bk), bk)

    def issue(slot, c, j):
        hh, r = chains[c]
        s_ref[slot, c] = _dot_nt(k_ref[0, hh, key_tile(j), :], q_ref[0, hh, r * sub:(r + 1) * sub, :])

    def consume(slot, c, j, masked):
        hh, _ = chains[c]
        s_t = s_ref[slot, c]
        if masked:
            key = lax.broadcasted_iota(jnp.int32, (bk, sub), 0)
            query = lax.broadcasted_iota(jnp.int32, (bk, sub), 1)
            s_t = jnp.where(key <= query, s_t, -jnp.inf)
        m = m_ref[c]
        m_new = jnp.maximum(m, jnp.max(s_t, axis=0, keepdims=True))
        p_t = jnp.exp2(s_t - m_new).astype(BF16)
        acc_ref[c] = jnp.exp2(m - m_new) * acc_ref[c] + _dot(vt_ref[0, hh, :, key_tile(j)], p_t)
        m_ref[c] = m_new

    m_ref[...] = jnp.full_like(m_ref, -jnp.inf)
    acc_ref[...] = jnp.zeros_like(acc_ref)
    for c in range(len(chains)):
        issue(0, c, 0)

    @pl.loop(0, i * (n_sub // 2))
    def _(u):
        for c in range(len(chains)):
            issue(1, c, 2 * u + 1)
            consume(0, c, 2 * u, False)
        for c in range(len(chains)):
            issue(0, c, 2 * u + 2)
            consume(1, c, 2 * u + 1, False)

    for t in range(n_sub):
        for c, (_, r) in enumerate(chains):
            if r > t:
                issue((t + 1) % 2, c, n_sub * i + t + 1)
            if r >= t:
                consume(t % 2, c, n_sub * i + t, masked=(r == t))

    for r in range(n_sub):
        out_t = [acc_ref[n_sub * hh + r, 0:FOX_DH, :] / acc_ref[n_sub * hh + r, FOX_DH:FOX_DH + 1, :]
                 for hh in range(2)]
        o_ref[0, r * sub:(r + 1) * sub, :] = jnp.concatenate(out_t, axis=0).T.astype(BF16)


def _fox_attn(q, k, vt):
    B, H, S, _ = q.shape
    assert FOX_BK == FOX_SUB and FOX_NSUB % 2 == 0
    n_chain = 2 * FOX_NSUB
    return pl.pallas_call(
        _fox_attn_kernel,
        grid=(B, H // 2, S // FOX_TQ),
        in_specs=[
            pl.BlockSpec((1, 2, FOX_TQ, LANES), lambda b, hp, i: (b, hp, i, 0)),
            pl.BlockSpec((1, 2, S, LANES), lambda b, hp, i: (b, hp, 0, 0)),
            pl.BlockSpec((1, 2, FOX_VT, S), lambda b, hp, i: (b, hp, 0, 0)),
        ],
        out_specs=pl.BlockSpec((1, FOX_TQ, LANES), lambda b, hp, i: (b, i, hp)),
        out_shape=jax.ShapeDtypeStruct((B, S, D_MODEL), BF16),
        scratch_shapes=[
            pltpu.VMEM((2, n_chain, FOX_BK, FOX_SUB), F32),
            pltpu.VMEM((n_chain, 1, FOX_SUB), F32),
            pltpu.VMEM((n_chain, FOX_VT, FOX_SUB), F32),
        ],
        compiler_params=_params("parallel", "parallel", "arbitrary"),
        name="fox_attn",
    )(q, k, vt)


def kernel(x, p, norm_w, ffn_w_in, ffn_w_out, ret_w_in, ret_gn_w, ret_w_out, fox_w_in, fox_b_f,
           fox_w_out, ple_w_proj, ple_w_gate, final_norm_w):
    B, S, D = x.shape
    T = B * S
    depth = norm_w.shape[0]
    ffn_w_in = ffn_w_in.astype(BF16)
    ffn_w_out = ffn_w_out.astype(BF16)
    ret_w_in = ret_w_in.astype(BF16)
    ret_w_out = ret_w_out.astype(BF16)
    fox_w_in_b = fox_w_in.astype(BF16)
    fox_w_out = fox_w_out.astype(BF16)
    ple_w_proj = ple_w_proj.astype(BF16)
    ple_w_gate = ple_w_gate.astype(BF16)
    norm_w = norm_w.reshape(depth, 4, 1, D)
    final_w = final_norm_w.reshape(1, D)
    p = p.reshape(depth, T, D_PLE)

    cos, sin, gamma_c, decay, zeta, xi = _retention_tables(S)
    place, ones = _fox_tables()

    gn_w = ret_gn_w.reshape(-1, RET_HEADS, 1, RET_DV)
    h = x.reshape(T, D)
    for i in range(depth):
        j = i // 2
        h = _ffn(h, norm_w, ffn_w_in, ffn_w_out, i, 0)
        mix = None
        if i % 2 == 0:
            q, k, v, g = _ret_proj(h, norm_w, ret_w_in, i, j, cos, sin, S)
            h = _ret_core(q, k, v, g, h, gamma_c, decay, zeta, xi, gn_w, ret_w_out, j, B, S)
        else:
            w_f = jnp.pad(fox_w_in_b[j][:, 3 * D:], ((0, 0), (0, LANES - FOX_HEADS)))
            b_f = jnp.pad(fox_b_f[j], (0, LANES - FOX_HEADS)).reshape(1, LANES)
            qa, ka, vt = _fox_proj(h.reshape(B, S, D), norm_w, fox_w_in_b, i, j, w_f, b_f,
                                   place, ones)
            mix = (_fox_attn(qa, ka, vt).reshape(T, D), fox_w_out, j)
        h = _ffn(h, norm_w, ffn_w_in, ffn_w_out, i, 1, mix=mix,
                 ple=(p, ple_w_gate, ple_w_proj),
                 final_w=final_w if i == depth - 1 else None)
    return h.reshape(B, S, D)
```

```python
import functools

import jax
import jax.numpy as jnp
from jax import lax
from jax.experimental import pallas as pl
from jax.experimental.pallas import tpu as pltpu

D_MODEL = 1024
D_PLE = 256
D_FF = 2816
EPS = 1e-6
RET_DK = 256
RET_HEADS = D_MODEL // RET_DK
RET_DV = 2 * RET_DK
RET_CHUNK = 256
ROPE_BASE = 10000.0
FOX_DH = 64
FOX_HEADS = D_MODEL // FOX_DH

LANES = 128
VMEM_LIMIT = 56 * 1024 * 1024

F32 = jnp.float32
BF16 = jnp.bfloat16

AUG_ONE = FOX_DH
AUG_C = FOX_DH + 3


def _rms(x, w):
    return x * lax.rsqrt(jnp.mean(x * x, axis=-1, keepdims=True) + EPS) * w


def _dot(a, b):
    return jnp.dot(a, b, preferred_element_type=F32)


def _dot_nt(a, b):
    return lax.dot_general(a, b, (((1,), (1,)), ((), ())), preferred_element_type=F32)


def _dot_tn(a, b):
    return lax.dot_general(a, b, (((0,), (0,)), ((), ())), preferred_element_type=F32)


def _resident(shape, index_map):
    return pl.BlockSpec(shape, index_map, pipeline_mode=pl.Buffered(1))


def _layer(arr, *lead, cols=None):
    tail = arr.shape[len(lead):]
    if cols is not None:
        tail = tail[:-1] + (cols,)
    idx = tuple(lead) + (0,) * len(tail)
    return _resident((None,) * len(lead) + tail, lambda *_: idx)


def _params(*sem):
    return pltpu.CompilerParams(dimension_semantics=sem, vmem_limit_bytes=VMEM_LIMIT)


FFN_TM = 512
FFN_TF = 256


def _ffn_kernel(*refs, mix, ple, final):
    refs = list(refs)
    o_ref = refs.pop()
    x_ref, nw_ref, win_ref, wout_ref = refs[:4]
    rest = refs[4:]
    x = x_ref[...]
    if mix:
        a_ref, wmix_ref = rest[:2]
        rest = rest[2:]
        x = x + _dot(a_ref[...], wmix_ref[...])
    xn = _rms(x, nw_ref[...]).astype(BF16)
    acc = jnp.zeros_like(x)
    for c in range(D_FF // FFN_TF):
        lo = c * FFN_TF
        g = _dot(xn, win_ref[:, lo:lo + FFN_TF])
        u = _dot(xn, win_ref[:, D_FF + lo:D_FF + lo + FFN_TF])
        a = (g * jax.nn.sigmoid(g) * u).astype(BF16)
        acc = acc + _dot(a, wout_ref[lo:lo + FFN_TF, :])
    h = x + 0.5 * acc
    if ple:
        p_ref, gnw_ref, wgate_ref, wproj_ref = rest[:4]
        rest = rest[4:]
        hn = _rms(h, gnw_ref[...]).astype(BF16)
        gate = jax.nn.sigmoid(_dot(hn, wgate_ref[...]))
        h = h + gate * _dot(p_ref[...].astype(BF16), wproj_ref[...])
    if final:
        h = _rms(h, rest[0][...])
    o_ref[...] = h


def _ffn(h, norm_w, w_in, w_out, i, k, mix=None, ple=None, final_w=None):
    T = h.shape[0]
    row = lambda t: (t, 0)
    in_specs = [
        pl.BlockSpec((FFN_TM, D_MODEL), row),
        _layer(norm_w, i, 2 * k),
        _layer(w_in, i, k),
        _layer(w_out, i, k),
    ]
    args = [h, norm_w, w_in, w_out]
    if mix is not None:
        a, wmix, j = mix
        in_specs += [pl.BlockSpec((FFN_TM, a.shape[1]), row), _layer(wmix, j)]
        args += [a, wmix]
    if ple is not None:
        p, wgate, wproj = ple
        in_specs += [
            pl.BlockSpec((None, FFN_TM, D_PLE), lambda t: (i, t, 0)),
            _layer(norm_w, i, 3),
            _layer(wgate, i),
            _layer(wproj, i),
        ]
        args += [p, norm_w, wgate, wproj]
    if final_w is not None:
        in_specs.append(_resident((1, D_MODEL), lambda t: (0, 0)))
        args.append(final_w)
    return pl.pallas_call(
        functools.partial(_ffn_kernel, mix=mix is not None, ple=ple is not None,
                          final=final_w is not None),
        grid=(T // FFN_TM,),
        in_specs=in_specs,
        out_specs=pl.BlockSpec((FFN_TM, D_MODEL), row),
        out_shape=jax.ShapeDtypeStruct((T, D_MODEL), F32),
        compiler_params=_params("parallel"),
        name="ffn",
    )(*args)


RETP_TM = 512
RET_QK = RET_HEADS * RET_DK
RET_V = RET_HEADS * RET_DV


def _ret_proj_kernel(x_ref, nw_ref, w_ref, cos_ref, sin_ref, q_ref, k_ref, v_ref, g_ref):
    xn = _rms(x_ref[...], nw_ref[...]).astype(BF16)
    cos = cos_ref[...]
    sin = sin_ref[...]
    half = RET_DK // 2

    def rotary(col, out_ref, scale):
        for h in range(RET_HEADS):
            y = _dot(xn, w_ref[:, col + h * RET_DK:col + (h + 1) * RET_DK])
            x1, x2 = y[:, :half], y[:, half:]
            lo = h * RET_DK
            out_ref[:, lo:lo + half] = ((x1 * cos - x2 * sin) * scale).astype(BF16)
            out_ref[:, lo + half:lo + RET_DK] = ((x1 * sin + x2 * cos) * scale).astype(BF16)

    rotary(0, q_ref, 1.0)
    rotary(RET_QK, k_ref, RET_DK ** -0.5)
    for h in range(RET_HEADS):
        lo = h * RET_DV
        v_ref[:, lo:lo + RET_DV] = _dot(xn, w_ref[:, 2 * RET_QK + lo:2 * RET_QK + lo + RET_DV]).astype(BF16)
        g_ref[:, lo:lo + RET_DV] = _dot(
            xn, w_ref[:, 2 * RET_QK + RET_V + lo:2 * RET_QK + RET_V + lo + RET_DV]).astype(BF16)


def _ret_proj(h, norm_w, w_in, i, j, cos, sin, seq):
    T = h.shape[0]
    row = lambda t: (t, 0)
    pos = lambda t: (t % (seq // RETP_TM), 0)
    return pl.pallas_call(
        _ret_proj_kernel,
        grid=(T // RETP_TM,),
        in_specs=[
            pl.BlockSpec((RETP_TM, D_MODEL), row),
            _layer(norm_w, i, 1),
            _layer(w_in, j),
            pl.BlockSpec((RETP_TM, RET_DK // 2), pos),
            pl.BlockSpec((RETP_TM, RET_DK // 2), pos),
        ],
        out_specs=[
            pl.BlockSpec((RETP_TM, RET_QK), row),
            pl.BlockSpec((RETP_TM, RET_QK), row),
            pl.BlockSpec((RETP_TM, RET_V), row),
            pl.BlockSpec((RETP_TM, RET_V), row),
        ],
        out_shape=[
            jax.ShapeDtypeStruct((T, RET_QK), BF16),
            jax.ShapeDtypeStruct((T, RET_QK), BF16),
            jax.ShapeDtypeStruct((T, RET_V), BF16),
            jax.ShapeDtypeStruct((T, RET_V), BF16),
        ],
        compiler_params=_params("parallel"),
        name="ret_proj",
    )(h, norm_w, w_in, cos, sin)


def _ret_core_kernel(gc_ref, q_ref, k_ref, v_ref, g_ref, res_ref, decay_ref, zeta_ref, xi_ref,
                     gn_ref, wo_ref, o_ref, state_ref):
    @pl.when(pl.program_id(1) == 0)
    def _():
        state_ref[...] = jnp.zeros_like(state_ref)

    heads = range(RET_HEADS)
    qs = [q_ref[:, h * RET_DK:(h + 1) * RET_DK] for h in heads]
    ks = [k_ref[:, h * RET_DK:(h + 1) * RET_DK] for h in heads]
    vs = [v_ref[:, h * RET_DV:(h + 1) * RET_DV] for h in heads]
    ss = [(_dot_nt(qs[h], ks[h]) * decay_ref[h]).astype(BF16) for h in heads]
    cross = [_dot(qs[h], state_ref[h].astype(BF16)) * xi_ref[h] for h in heads]
    outs = [_dot(ss[h], vs[h]) + cross[h] for h in heads]
    for h in heads:
        kz = (ks[h].astype(F32) * zeta_ref[h]).astype(BF16)
        state_ref[h] = gc_ref[h] * state_ref[h] + _dot_tn(kz, vs[h])
    acc = res_ref[...]
    for h in heads:
        gh = g_ref[:, h * RET_DV:(h + 1) * RET_DV].astype(F32)
        y = (gh * jax.nn.sigmoid(gh) * _rms(outs[h], gn_ref[h])).astype(BF16)
        acc = acc + _dot(y, wo_ref[h * RET_DV:(h + 1) * RET_DV, :])
    o_ref[...] = acc


def _ret_core(q, k, v, g, res, gamma_c, decay, zeta, xi, gn_w, w_out, j, batch, seq):
    T = q.shape[0]
    C = RET_CHUNK
    nc = seq // C
    row = lambda b, c: (b * nc + c, 0)
    fixed3 = lambda b, c: (0, 0, 0)
    return pl.pallas_call(
        _ret_core_kernel,
        grid=(batch, nc),
        in_specs=[
            pl.BlockSpec(memory_space=pltpu.SMEM),
            pl.BlockSpec((C, RET_QK), row),
            pl.BlockSpec((C, RET_QK), row),
            pl.BlockSpec((C, RET_V), row),
            pl.BlockSpec((C, RET_V), row),
            pl.BlockSpec((C, D_MODEL), row),
            _resident((RET_HEADS, C, C), fixed3),
            _resident((RET_HEADS, C, 1), fixed3),
            _resident((RET_HEADS, C, 1), fixed3),
            _layer(gn_w, j),
            _layer(w_out, j),
        ],
        out_specs=pl.BlockSpec((C, D_MODEL), row),
        out_shape=jax.ShapeDtypeStruct((T, D_MODEL), F32),
        scratch_shapes=[pltpu.VMEM((RET_HEADS, RET_DK, RET_DV), F32)],
        compiler_params=_params("parallel", "arbitrary"),
        name="ret_core",
    )(gamma_c, q, k, v, g, res, decay, zeta, xi, gn_w, w_out)


def _retention_tables(seq):
    H, C = RET_HEADS, RET_CHUNK
    half = RET_DK // 2
    pos = jnp.arange(seq, dtype=F32)
    inv_freq = ROPE_BASE ** (-jnp.arange(half, dtype=F32) / half)
    ang = pos[:, None] * inv_freq[None, :]
    log_gamma = jnp.log1p(-jnp.exp2(-5.0 - jnp.arange(H, dtype=F32)))
    idx = jnp.arange(C, dtype=F32)
    diff = idx[:, None] - idx[None, :]
    decay = jnp.where(diff[None] >= 0,
                      jnp.exp(log_gamma[:, None, None] * jnp.maximum(diff, 0.0)[None]), 0.0)
    zeta = jnp.exp(log_gamma[:, None] * (C - 1 - idx)[None, :])[:, :, None]
    xi = jnp.exp(log_gamma[:, None] * (idx + 1)[None, :])[:, :, None]
    gamma_c = jnp.exp(log_gamma * C)
    return jnp.cos(ang), jnp.sin(ang), gamma_c, decay, zeta, xi


FOXP_TM = 512
FOX_SUB = 256
FOX_NSUB = 4
FOX_TQ = FOX_NSUB * FOX_SUB
FOX_BK = 256
FOX_VT = FOX_DH + 16
LOG2E = 1.4426950408889634


def _cumsum_rows(x):
    n = x.shape[0]
    row = lax.broadcasted_iota(jnp.int32, x.shape, 0)
    shift = 1
    while shift < n:
        x = x + jnp.where(row >= shift, pltpu.roll(x, shift, axis=0), 0.0)
        shift *= 2
    return x


def _split3(c):
    hi = c.astype(BF16).astype(F32)
    r = c - hi
    mid = r.astype(BF16).astype(F32)
    lo = (r - mid).astype(BF16).astype(F32)
    return hi, mid, lo


def _fox_proj_kernel(x_ref, nw_ref, w_ref, wf_ref, bf_ref, place_ref,
                     q_ref, k_ref, vt_ref, carry_ref):
    @pl.when(pl.program_id(1) == 0)
    def _():
        carry_ref[...] = jnp.zeros_like(carry_ref)

    xn = _rms(x_ref[0], nw_ref[...]).astype(BF16)
    tm = xn.shape[0]

    z = _dot(xn, wf_ref[...]) + bf_ref[...]
    log_f = jnp.minimum(z, 0.0) - jnp.log(1.0 + jnp.exp(-jnp.abs(z)))
    c = _cumsum_rows(log_f) + carry_ref[...]
    carry_ref[...] = c[tm - 1:tm, :]

    hi, mid, lo = _split3(c * LOG2E)
    lane = lax.broadcasted_iota(jnp.int32, c.shape, 1)
    parts = jnp.where(lane < FOX_HEADS, hi,
                      jnp.where(lane < 2 * FOX_HEADS, pltpu.roll(mid, FOX_HEADS, axis=1),
                                pltpu.roll(lo, 2 * FOX_HEADS, axis=1))).astype(BF16)

    parts6 = _dot(parts, place_ref[...])

    dh = FOX_DH
    nqk = FOX_HEADS * dh
    lane_t = lax.broadcasted_iota(jnp.int32, (tm, LANES), 1)
    is_feat, is_one, is_c = lane_t < dh, lane_t < AUG_C, lane_t < AUG_C + 3
    yq_all = _dot(xn, w_ref[:, 0:nqk]) * (dh ** -0.5 * LOG2E)
    yk_all = _dot(xn, w_ref[:, nqk:2 * nqk])
    for h in range(FOX_HEADS):
        pair, hh = divmod(h, 2)
        c6 = pltpu.roll(parts6, (AUG_ONE - 6 * h) % LANES, axis=1)
        yq = yq_all[:, pair * LANES:(pair + 1) * LANES]
        yk = yk_all[:, pair * LANES:(pair + 1) * LANES]
        if hh == 1:
            yq, yk = pltpu.roll(yq, dh, axis=1), pltpu.roll(yk, dh, axis=1)
        aug_q = jnp.where(is_one, 1.0, jnp.where(is_c, c6, 0.0))
        aug_k = jnp.where(is_one, -c6, jnp.where(is_c, 1.0, 0.0))
        q_ref[0, h] = jnp.where(is_feat, yq, aug_q).astype(BF16)
        k_ref[0, h] = jnp.where(is_feat, yk, aug_k).astype(BF16)

    sub_t = lax.broadcasted_iota(jnp.int32, (FOX_VT - dh, tm), 0)
    v_aug = jnp.where(sub_t == 0, 1.0, 0.0).astype(BF16)
    y_all = _dot(xn, w_ref[:, 2 * nqk:3 * nqk])
    for pair in range(FOX_HEADS // 2):
        yt = y_all[:, pair * LANES:(pair + 1) * LANES].T
        for hh in range(2):
            vt_ref[0, 2 * pair + hh, 0:dh, :] = yt[hh * dh:(hh + 1) * dh, :].astype(BF16)
            vt_ref[0, 2 * pair + hh, dh:FOX_VT, :] = v_aug


def _fox_place():
    rows = jnp.arange(LANES)[:, None]
    cols = jnp.arange(LANES)[None, :]
    head, part = cols // 6, cols % 3
    return ((cols < 6 * FOX_HEADS) & (rows == part * FOX_HEADS + head)).astype(BF16)


def _fox_proj(h3, norm_w, w_in, i, j, w_f, b_f, place):
    B, S, _ = h3.shape
    H = FOX_HEADS
    fixed = lambda b, t: (0, 0)
    return pl.pallas_call(
        _fox_proj_kernel,
        grid=(B, S // FOXP_TM),
        in_specs=[
            pl.BlockSpec((1, FOXP_TM, D_MODEL), lambda b, t: (b, t, 0)),
            _layer(norm_w, i, 1),
            _layer(w_in, j, cols=3 * D_MODEL),
            _resident((D_MODEL, LANES), fixed),
            _resident((1, LANES), fixed),
            _resident((LANES, LANES), fixed),
        ],
        out_specs=[
            pl.BlockSpec((1, H, FOXP_TM, LANES), lambda b, t: (b, 0, t, 0)),
            pl.BlockSpec((1, H, FOXP_TM, LANES), lambda b, t: (b, 0, t, 0)),
            pl.BlockSpec((1, H, FOX_VT, FOXP_TM), lambda b, t: (b, 0, 0, t)),
        ],
        out_shape=[jax.ShapeDtypeStruct((B, H, S, LANES), BF16)] * 2
        + [jax.ShapeDtypeStruct((B, H, FOX_VT, S), BF16)],
        scratch_shapes=[pltpu.VMEM((1, LANES), F32)],
        compiler_params=_params("parallel", "arbitrary"),
        name="fox_proj",
    )(h3, norm_w, w_in, w_f, b_f, place)


def _fox_attn_kernel(q_ref, k_ref, vt_ref, o_ref, s_ref, m_ref, acc_ref):
    i = pl.program_id(2)
    sub, bk, n_sub = FOX_SUB, FOX_BK, FOX_NSUB
    chains = [(hh, r) for hh in range(2) for r in range(n_sub)]

    def key_tile(j):
        return pl.ds(pl.multiple_of(j * bk, bk), bk)

    def issue(slot, c, j):
        hh, r = chains[c]
        s_ref[slot, c] = _dot_nt(k_ref[0, hh, key_tile(j), :], q_ref[0, hh, r * sub:(r + 1) * sub, :])

    def consume(slot, c, j, masked):
        hh, _ = chains[c]
        s_t = s_ref[slot, c]
        if masked:
            key = lax.broadcasted_iota(jnp.int32, (bk, sub), 0)
            query = lax.broadcasted_iota(jnp.int32, (bk, sub), 1)
            s_t = jnp.where(key <= query, s_t, -jnp.inf)
        m = m_ref[c]
        m_new = jnp.maximum(m, jnp.max(s_t, axis=0, keepdims=True))
        p_t = jnp.exp2(s_t - m_new).astype(BF16)
        acc_ref[c] = jnp.exp2(m - m_new) * acc_ref[c] + _dot(vt_ref[0, hh, :, key_tile(j)], p_t)
        m_ref[c] = m_new

    m_ref[...] = jnp.full_like(m_ref, -jnp.inf)
    acc_ref[...] = jnp.zeros_like(acc_ref)
    for c in range(len(chains)):
        issue(0, c, 0)

    @pl.loop(0, i)
    def _(u):
        for t in range(n_sub):
            for c in range(len(chains)):
                issue((t + 1) % 2, c, n_sub * u + t + 1)
                consume(t % 2, c, n_sub * u + t, False)

    for t in range(n_sub):
        for c, (_, r) in enumerate(chains):
            if r > t:
                issue((t + 1) % 2, c, n_sub * i + t + 1)
            if r >= t:
                consume(t % 2, c, n_sub * i + t, masked=(r == t))

    for r in range(n_sub):
        out_t = [acc_ref[n_sub * hh + r, 0:FOX_DH, :] / acc_ref[n_sub * hh + r, FOX_DH:FOX_DH + 1, :]
                 for hh in range(2)]
        o_ref[0, r * sub:(r + 1) * sub, :] = jnp.concatenate(out_t, axis=0).T.astype(BF16)


def _fox_attn(q, k, vt):
    B, H, S, _ = q.shape
    assert FOX_BK == FOX_SUB and FOX_NSUB % 2 == 0
    n_chain = 2 * FOX_NSUB
    return pl.pallas_call(
        _fox_attn_kernel,
        grid=(B, H // 2, S // FOX_TQ),
        in_specs=[
            pl.BlockSpec((1, 2, FOX_TQ, LANES), lambda b, hp, i: (b, hp, i, 0)),
            pl.BlockSpec((1, 2, S, LANES), lambda b, hp, i: (b, hp, 0, 0)),
            pl.BlockSpec((1, 2, FOX_VT, S), lambda b, hp, i: (b, hp, 0, 0)),
        ],
        out_specs=pl.BlockSpec((1, FOX_TQ, LANES), lambda b, hp, i: (b, i, hp)),
        out_shape=jax.ShapeDtypeStruct((B, S, D_MODEL), BF16),
        scratch_shapes=[
            pltpu.VMEM((2, n_chain, FOX_BK, FOX_SUB), F32),
            pltpu.VMEM((n_chain, 1, FOX_SUB), F32),
            pltpu.VMEM((n_chain, FOX_VT, FOX_SUB), F32),
        ],
        compiler_params=_params("parallel", "parallel", "arbitrary"),
        name="fox_attn",
    )(q, k, vt)


def kernel(x, p, norm_w, ffn_w_in, ffn_w_out, ret_w_in, ret_gn_w, ret_w_out, fox_w_in, fox_b_f,
           fox_w_out, ple_w_proj, ple_w_gate, final_norm_w):
    B, S, D = x.shape
    T = B * S
    depth = norm_w.shape[0]
    ffn_w_in = ffn_w_in.astype(BF16)
    ffn_w_out = ffn_w_out.astype(BF16)
    ret_w_in = ret_w_in.astype(BF16)
    ret_w_out = ret_w_out.astype(BF16)
    fox_w_in_b = fox_w_in.astype(BF16)
    fox_w_out = fox_w_out.astype(BF16)
    ple_w_proj = ple_w_proj.astype(BF16)
    ple_w_gate = ple_w_gate.astype(BF16)
    norm_w = norm_w.reshape(depth, 4, 1, D)
    final_w = final_norm_w.reshape(1, D)
    p = p.reshape(depth, T, D_PLE)

    cos, sin, gamma_c, decay, zeta, xi = _retention_tables(S)
    place = _fox_place()

    gn_w = ret_gn_w.reshape(-1, RET_HEADS, 1, RET_DV)
    h = x.reshape(T, D)
    for i in range(depth):
        j = i // 2
        h = _ffn(h, norm_w, ffn_w_in, ffn_w_out, i, 0)
        mix = None
        if i % 2 == 0:
            q, k, v, g = _ret_proj(h, norm_w, ret_w_in, i, j, cos, sin, S)
            h = _ret_core(q, k, v, g, h, gamma_c, decay, zeta, xi, gn_w, ret_w_out, j, B, S)
        else:
            w_f = jnp.pad(fox_w_in_b[j][:, 3 * D:], ((0, 0), (0, LANES - FOX_HEADS)))
            b_f = jnp.pad(fox_b_f[j], (0, LANES - FOX_HEADS)).reshape(1, LANES)
            qa, ka, vt = _fox_proj(h.reshape(B, S, D), norm_w, fox_w_in_b, i, j, w_f, b_f, place)
            mix = (_fox_attn(qa, ka, vt).reshape(T, D), fox_w_out, j)
        h = _ffn(h, norm_w, ffn_w_in, ffn_w_out, i, 1, mix=mix,
                 ple=(p, ple_w_gate, ple_w_proj),
                 final_w=final_w if i == depth - 1 else None)
    return h.reshape(B, S, D)
```

```python
import functools

import jax
import jax.numpy as jnp
from jax import lax
from jax.experimental import pallas as pl
from jax.experimental.pallas import tpu as pltpu

D_MODEL = 1024
D_PLE = 256
D_FF = 2816
EPS = 1e-6
RET_DK = 256
RET_HEADS = D_MODEL // RET_DK
RET_DV = 2 * RET_DK
RET_CHUNK = 256
ROPE_BASE = 10000.0
FOX_DH = 64
FOX_HEADS = D_MODEL // FOX_DH

LANES = 128
VMEM_LIMIT = 56 * 1024 * 1024

F32 = jnp.float32
BF16 = jnp.bfloat16

AUG_ONE = FOX_DH
AUG_C = FOX_DH + 3


def _rms(x, w):
    return x * lax.rsqrt(jnp.mean(x * x, axis=-1, keepdims=True) + EPS) * w


def _dot(a, b):
    return jnp.dot(a, b, preferred_element_type=F32)


def _dot_nt(a, b):
    return lax.dot_general(a, b, (((1,), (1,)), ((), ())), preferred_element_type=F32)


def _dot_tn(a, b):
    return lax.dot_general(a, b, (((0,), (0,)), ((), ())), preferred_element_type=F32)


def _resident(shape, index_map):
    return pl.BlockSpec(shape, index_map, pipeline_mode=pl.Buffered(1))


def _layer(arr, *lead, cols=None):
    tail = arr.shape[len(lead):]
    if cols is not None:
        tail = tail[:-1] + (cols,)
    idx = tuple(lead) + (0,) * len(tail)
    return _resident((None,) * len(lead) + tail, lambda *_: idx)


def _params(*sem):
    return pltpu.CompilerParams(dimension_semantics=sem, vmem_limit_bytes=VMEM_LIMIT)


FFN_TM = 1024
FFN_TF = 256


def _ffn_kernel(*refs, mix, ple, final):
    refs = list(refs)
    o_ref = refs.pop()
    x_ref, nw_ref, win_ref, wout_ref = refs[:4]
    rest = refs[4:]
    x = x_ref[...]
    if mix:
        a_ref, wmix_ref = rest[:2]
        rest = rest[2:]
        x = x + _dot(a_ref[...], wmix_ref[...])
    xn = _rms(x, nw_ref[...]).astype(BF16)
    acc = jnp.zeros_like(x)
    for c in range(D_FF // FFN_TF):
        lo = c * FFN_TF
        g = _dot(xn, win_ref[:, lo:lo + FFN_TF])
        u = _dot(xn, win_ref[:, D_FF + lo:D_FF + lo + FFN_TF])
        a = (g * jax.nn.sigmoid(g) * u).astype(BF16)
        acc = acc + _dot(a, wout_ref[lo:lo + FFN_TF, :])
    h = x + 0.5 * acc
    if ple:
        p_ref, gnw_ref, wgate_ref, wproj_ref = rest[:4]
        rest = rest[4:]
        hn = _rms(h, gnw_ref[...]).astype(BF16)
        gate = jax.nn.sigmoid(_dot(hn, wgate_ref[...]))
        h = h + gate * _dot(p_ref[...].astype(BF16), wproj_ref[...])
    if final:
        h = _rms(h, rest[0][...])
    o_ref[...] = h


def _ffn(h, norm_w, w_in, w_out, i, k, mix=None, ple=None, final_w=None):
    T = h.shape[0]
    tm = FFN_TM if mix is None else FFN_TM // 2
    row = lambda t: (t, 0)
    in_specs = [
        pl.BlockSpec((tm, D_MODEL), row),
        _layer(norm_w, i, 2 * k),
        _layer(w_in, i, k),
        _layer(w_out, i, k),
    ]
    args = [h, norm_w, w_in, w_out]
    if mix is not None:
        a, wmix, j = mix
        in_specs += [pl.BlockSpec((tm, a.shape[1]), row), _layer(wmix, j)]
        args += [a, wmix]
    if ple is not None:
        p, wgate, wproj = ple
        in_specs += [
            pl.BlockSpec((None, tm, D_PLE), lambda t: (i, t, 0)),
            _layer(norm_w, i, 3),
            _layer(wgate, i),
            _layer(wproj, i),
        ]
        args += [p, norm_w, wgate, wproj]
    if final_w is not None:
        in_specs.append(_resident((1, D_MODEL), lambda t: (0, 0)))
        args.append(final_w)
    return pl.pallas_call(
        functools.partial(_ffn_kernel, mix=mix is not None, ple=ple is not None,
                          final=final_w is not None),
        grid=(T // tm,),
        in_specs=in_specs,
        out_specs=pl.BlockSpec((tm, D_MODEL), row),
        out_shape=jax.ShapeDtypeStruct((T, D_MODEL), F32),
        compiler_params=_params("parallel"),
        name="ffn",
    )(*args)


RETP_TM = 512
RET_QK = RET_HEADS * RET_DK
RET_V = RET_HEADS * RET_DV


def _ret_proj_kernel(x_ref, nw_ref, w_ref, cos_ref, sin_ref, q_ref, k_ref, v_ref, g_ref):
    xn = _rms(x_ref[...], nw_ref[...]).astype(BF16)
    cos = cos_ref[...]
    sin = sin_ref[...]
    half = RET_DK // 2

    def rotary(col, out_ref, scale):
        for h in range(RET_HEADS):
            y = _dot(xn, w_ref[:, col + h * RET_DK:col + (h + 1) * RET_DK])
            x1, x2 = y[:, :half], y[:, half:]
            lo = h * RET_DK
            out_ref[:, lo:lo + half] = ((x1 * cos - x2 * sin) * scale).astype(BF16)
            out_ref[:, lo + half:lo + RET_DK] = ((x1 * sin + x2 * cos) * scale).astype(BF16)

    rotary(0, q_ref, 1.0)
    rotary(RET_QK, k_ref, RET_DK ** -0.5)
    for h in range(RET_HEADS):
        lo = h * RET_DV
        v_ref[:, lo:lo + RET_DV] = _dot(xn, w_ref[:, 2 * RET_QK + lo:2 * RET_QK + lo + RET_DV]).astype(BF16)
        g_ref[:, lo:lo + RET_DV] = _dot(
            xn, w_ref[:, 2 * RET_QK + RET_V + lo:2 * RET_QK + RET_V + lo + RET_DV]).astype(BF16)


def _ret_proj(h, norm_w, w_in, i, j, cos, sin, seq):
    T = h.shape[0]
    row = lambda t: (t, 0)
    pos = lambda t: (t % (seq // RETP_TM), 0)
    return pl.pallas_call(
        _ret_proj_kernel,
        grid=(T // RETP_TM,),
        in_specs=[
            pl.BlockSpec((RETP_TM, D_MODEL), row),
            _layer(norm_w, i, 1),
            _layer(w_in, j),
            pl.BlockSpec((RETP_TM, RET_DK // 2), pos),
            pl.BlockSpec((RETP_TM, RET_DK // 2), pos),
        ],
        out_specs=[
            pl.BlockSpec((RETP_TM, RET_QK), row),
            pl.BlockSpec((RETP_TM, RET_QK), row),
            pl.BlockSpec((RETP_TM, RET_V), row),
            pl.BlockSpec((RETP_TM, RET_V), row),
        ],
        out_shape=[
            jax.ShapeDtypeStruct((T, RET_QK), BF16),
            jax.ShapeDtypeStruct((T, RET_QK), BF16),
            jax.ShapeDtypeStruct((T, RET_V), BF16),
            jax.ShapeDtypeStruct((T, RET_V), BF16),
        ],
        compiler_params=_params("parallel"),
        name="ret_proj",
    )(h, norm_w, w_in, cos, sin)


def _ret_core_kernel(gc_ref, q_ref, k_ref, v_ref, g_ref, res_ref, decay_ref, zeta_ref, xi_ref,
                     gn_ref, wo_ref, o_ref, state_ref):
    @pl.when(pl.program_id(1) == 0)
    def _():
        state_ref[...] = jnp.zeros_like(state_ref)

    heads = range(RET_HEADS)
    qs = [q_ref[:, h * RET_DK:(h + 1) * RET_DK] for h in heads]
    ks = [k_ref[:, h * RET_DK:(h + 1) * RET_DK] for h in heads]
    vs = [v_ref[:, h * RET_DV:(h + 1) * RET_DV] for h in heads]
    ss = [(_dot_nt(qs[h], ks[h]) * decay_ref[h]).astype(BF16) for h in heads]
    cross = [_dot(qs[h], state_ref[h].astype(BF16)) * xi_ref[h] for h in heads]
    outs = [_dot(ss[h], vs[h]) + cross[h] for h in heads]
    for h in heads:
        kz = (ks[h].astype(F32) * zeta_ref[h]).astype(BF16)
        state_ref[h] = gc_ref[h] * state_ref[h] + _dot_tn(kz, vs[h])
    acc = res_ref[...]
    for h in heads:
        gh = g_ref[:, h * RET_DV:(h + 1) * RET_DV].astype(F32)
        y = (gh * jax.nn.sigmoid(gh) * _rms(outs[h], gn_ref[h])).astype(BF16)
        acc = acc + _dot(y, wo_ref[h * RET_DV:(h + 1) * RET_DV, :])
    o_ref[...] = acc


def _ret_core(q, k, v, g, res, gamma_c, decay, zeta, xi, gn_w, w_out, j, batch, seq):
    T = q.shape[0]
    C = RET_CHUNK
    nc = seq // C
    row = lambda b, c: (b * nc + c, 0)
    fixed3 = lambda b, c: (0, 0, 0)
    return pl.pallas_call(
        _ret_core_kernel,
        grid=(batch, nc),
        in_specs=[
            pl.BlockSpec(memory_space=pltpu.SMEM),
            pl.BlockSpec((C, RET_QK), row),
            pl.BlockSpec((C, RET_QK), row),
            pl.BlockSpec((C, RET_V), row),
            pl.BlockSpec((C, RET_V), row),
            pl.BlockSpec((C, D_MODEL), row),
            _resident((RET_HEADS, C, C), fixed3),
            _resident((RET_HEADS, C, 1), fixed3),
            _resident((RET_HEADS, C, 1), fixed3),
            _layer(gn_w, j),
            _layer(w_out, j),
        ],
        out_specs=pl.BlockSpec((C, D_MODEL), row),
        out_shape=jax.ShapeDtypeStruct((T, D_MODEL), F32),
        scratch_shapes=[pltpu.VMEM((RET_HEADS, RET_DK, RET_DV), F32)],
        compiler_params=_params("parallel", "arbitrary"),
        name="ret_core",
    )(gamma_c, q, k, v, g, res, decay, zeta, xi, gn_w, w_out)


def _retention_tables(seq):
    H, C = RET_HEADS, RET_CHUNK
    half = RET_DK // 2
    pos = jnp.arange(seq, dtype=F32)
    inv_freq = ROPE_BASE ** (-jnp.arange(half, dtype=F32) / half)
    ang = pos[:, None] * inv_freq[None, :]
    log_gamma = jnp.log1p(-jnp.exp2(-5.0 - jnp.arange(H, dtype=F32)))
    idx = jnp.arange(C, dtype=F32)
    diff = idx[:, None] - idx[None, :]
    decay = jnp.where(diff[None] >= 0,
                      jnp.exp(log_gamma[:, None, None] * jnp.maximum(diff, 0.0)[None]), 0.0)
    zeta = jnp.exp(log_gamma[:, None] * (C - 1 - idx)[None, :])[:, :, None]
    xi = jnp.exp(log_gamma[:, None] * (idx + 1)[None, :])[:, :, None]
    gamma_c = jnp.exp(log_gamma * C)
    return jnp.cos(ang), jnp.sin(ang), gamma_c, decay, zeta, xi


FOXP_TM = 512
FOX_SUB = 256
FOX_NSUB = 4
FOX_TQ = FOX_NSUB * FOX_SUB
FOX_BK = 256
FOX_VT = FOX_DH + 16
LOG2E = 1.4426950408889634


def _cumsum_rows(x):
    n = x.shape[0]
    row = lax.broadcasted_iota(jnp.int32, x.shape, 0)
    shift = 1
    while shift < n:
        x = x + jnp.where(row >= shift, pltpu.roll(x, shift, axis=0), 0.0)
        shift *= 2
    return x


def _split3(c):
    hi = c.astype(BF16).astype(F32)
    r = c - hi
    mid = r.astype(BF16).astype(F32)
    lo = (r - mid).astype(BF16).astype(F32)
    return hi, mid, lo


def _fox_proj_kernel(x_ref, nw_ref, w_ref, wf_ref, bf_ref, place_ref,
                     q_ref, k_ref, vt_ref, carry_ref):
    @pl.when(pl.program_id(1) == 0)
    def _():
        carry_ref[...] = jnp.zeros_like(carry_ref)

    xn = _rms(x_ref[0], nw_ref[...]).astype(BF16)
    tm = xn.shape[0]

    z = _dot(xn, wf_ref[...]) + bf_ref[...]
    log_f = jnp.minimum(z, 0.0) - jnp.log(1.0 + jnp.exp(-jnp.abs(z)))
    c = _cumsum_rows(log_f) + carry_ref[...]
    carry_ref[...] = c[tm - 1:tm, :]

    hi, mid, lo = _split3(c * LOG2E)
    lane = lax.broadcasted_iota(jnp.int32, c.shape, 1)
    parts = jnp.where(lane < FOX_HEADS, hi,
                      jnp.where(lane < 2 * FOX_HEADS, pltpu.roll(mid, FOX_HEADS, axis=1),
                                pltpu.roll(lo, 2 * FOX_HEADS, axis=1))).astype(BF16)

    parts6 = _dot(parts, place_ref[...])

    dh = FOX_DH
    nqk = FOX_HEADS * dh
    lane_t = lax.broadcasted_iota(jnp.int32, (tm, LANES), 1)
    is_feat, is_one, is_c = lane_t < dh, lane_t < AUG_C, lane_t < AUG_C + 3
    yq_all = _dot(xn, w_ref[:, 0:nqk]) * (dh ** -0.5 * LOG2E)
    yk_all = _dot(xn, w_ref[:, nqk:2 * nqk])
    for h in range(FOX_HEADS):
        pair, hh = divmod(h, 2)
        c6 = pltpu.roll(parts6, (AUG_ONE - 6 * h) % LANES, axis=1)
        yq = yq_all[:, pair * LANES:(pair + 1) * LANES]
        yk = yk_all[:, pair * LANES:(pair + 1) * LANES]
        if hh == 1:
            yq, yk = pltpu.roll(yq, dh, axis=1), pltpu.roll(yk, dh, axis=1)
        aug_q = jnp.where(is_one, 1.0, jnp.where(is_c, c6, 0.0))
        aug_k = jnp.where(is_one, -c6, jnp.where(is_c, 1.0, 0.0))
        q_ref[0, h] = jnp.where(is_feat, yq, aug_q).astype(BF16)
        k_ref[0, h] = jnp.where(is_feat, yk, aug_k).astype(BF16)

    sub_t = lax.broadcasted_iota(jnp.int32, (FOX_VT - dh, tm), 0)
    v_aug = jnp.where(sub_t == 0, 1.0, 0.0).astype(BF16)
    y_all = _dot(xn, w_ref[:, 2 * nqk:3 * nqk])
    for pair in range(FOX_HEADS // 2):
        yt = y_all[:, pair * LANES:(pair + 1) * LANES].T
        for hh in range(2):
            vt_ref[0, 2 * pair + hh, 0:dh, :] = yt[hh * dh:(hh + 1) * dh, :].astype(BF16)
            vt_ref[0, 2 * pair + hh, dh:FOX_VT, :] = v_aug


def _fox_place():
    rows = jnp.arange(LANES)[:, None]
    cols = jnp.arange(LANES)[None, :]
    head, part = cols // 6, cols % 3
    return ((cols < 6 * FOX_HEADS) & (rows == part * FOX_HEADS + head)).astype(BF16)


def _fox_proj(h3, norm_w, w_in, i, j, w_f, b_f, place):
    B, S, _ = h3.shape
    H = FOX_HEADS
    fixed = lambda b, t: (0, 0)
    return pl.pallas_call(
        _fox_proj_kernel,
        grid=(B, S // FOXP_TM),
        in_specs=[
            pl.BlockSpec((1, FOXP_TM, D_MODEL), lambda b, t: (b, t, 0)),
            _layer(norm_w, i, 1),
            _layer(w_in, j, cols=3 * D_MODEL),
            _resident((D_MODEL, LANES), fixed),
            _resident((1, LANES), fixed),
            _resident((LANES, LANES), fixed),
        ],
        out_specs=[
            pl.BlockSpec((1, H, FOXP_TM, LANES), lambda b, t: (b, 0, t, 0)),
            pl.BlockSpec((1, H, FOXP_TM, LANES), lambda b, t: (b, 0, t, 0)),
            pl.BlockSpec((1, H, FOX_VT, FOXP_TM), lambda b, t: (b, 0, 0, t)),
        ],
        out_shape=[jax.ShapeDtypeStruct((B, H, S, LANES), BF16)] * 2
        + [jax.ShapeDtypeStruct((B, H, FOX_VT, S), BF16)],
        scratch_shapes=[pltpu.VMEM((1, LANES), F32)],
        compiler_params=_params("parallel", "arbitrary"),
        name="fox_proj",
    )(h3, norm_w, w_in, w_f, b_f, place)


def _fox_attn_kernel(q_ref, k_ref, vt_ref, o_ref, s_ref, m_ref, acc_ref):
    i = pl.program_id(2)
    sub, bk, n_sub = FOX_SUB, FOX_BK, FOX_NSUB
    chains = [(hh, r) for hh in range(2) for r in range(n_sub)]

    def key_tile(j):
        return pl.ds(pl.multiple_of(j * bk, bk), bk)

    def issue(slot, c, j):
        hh, r = chains[c]
        s_ref[slot, c] = _dot_nt(k_ref[0, hh, key_tile(j), :], q_ref[0, hh, r * sub:(r + 1) * sub, :])

    def consume(slot, c, j, masked):
        hh, _ = chains[c]
        s_t = s_ref[slot, c]
        if masked:
            key = lax.broadcasted_iota(jnp.int32, (bk, sub), 0)
            query = lax.broadcasted_iota(jnp.int32, (bk, sub), 1)
            s_t = jnp.where(key <= query, s_t, -jnp.inf)
        m = m_ref[c]
        m_new = jnp.maximum(m, jnp.max(s_t, axis=0, keepdims=True))
        p_t = jnp.exp2(s_t - m_new).astype(BF16)
        acc_ref[c] = jnp.exp2(m - m_new) * acc_ref[c] + _dot(vt_ref[0, hh, :, key_tile(j)], p_t)
        m_ref[c] = m_new

    m_ref[...] = jnp.full_like(m_ref, -jnp.inf)
    acc_ref[...] = jnp.zeros_like(acc_ref)

    for c in range(len(chains)):
        issue(0, c, 0)

    @pl.loop(0, i)
    def _(u):
        for t in range(n_sub):
            for c in range(len(chains)):
                issue((t + 1) % 2, c, n_sub * u + t + 1)
                consume(t % 2, c, n_sub * u + t, False)

    for t in range(n_sub):
        for c, (_, r) in enumerate(chains):
            if r > t:
                issue((t + 1) % 2, c, n_sub * i + t + 1)
            if r >= t:
                consume(t % 2, c, n_sub * i + t, masked=(r == t))

    for r in range(n_sub):
        out_t = [acc_ref[n_sub * hh + r, 0:FOX_DH, :] / acc_ref[n_sub * hh + r, FOX_DH:FOX_DH + 1, :]
                 for hh in range(2)]
        o_ref[0, r * sub:(r + 1) * sub, :] = jnp.concatenate(out_t, axis=0).T.astype(BF16)


def _fox_attn(q, k, vt):
    B, H, S, _ = q.shape
    assert FOX_BK == FOX_SUB and FOX_NSUB % 2 == 0
    n_chain = 2 * FOX_NSUB
    return pl.pallas_call(
        _fox_attn_kernel,
        grid=(B, H // 2, S // FOX_TQ),
        in_specs=[
            pl.BlockSpec((1, 2, FOX_TQ, LANES), lambda b, hp, i: (b, hp, i, 0)),
            pl.BlockSpec((1, 2, S, LANES), lambda b, hp, i: (b, hp, 0, 0)),
            pl.BlockSpec((1, 2, FOX_VT, S), lambda b, hp, i: (b, hp, 0, 0)),
        ],
        out_specs=pl.BlockSpec((1, FOX_TQ, LANES), lambda b, hp, i: (b, i, hp)),
        out_shape=jax.ShapeDtypeStruct((B, S, D_MODEL), BF16),
        scratch_shapes=[
            pltpu.VMEM((2, n_chain, FOX_BK, FOX_SUB), F32),
            pltpu.VMEM((n_chain, 1, FOX_SUB), F32),
            pltpu.VMEM((n_chain, FOX_VT, FOX_SUB), F32),
        ],
        compiler_params=_params("parallel", "parallel", "arbitrary"),
        name="fox_attn",
    )(q, k, vt)


def kernel(x, p, norm_w, ffn_w_in, ffn_w_out, ret_w_in, ret_gn_w, ret_w_out, fox_w_in, fox_b_f,
           fox_w_out, ple_w_proj, ple_w_gate, final_norm_w):
    B, S, D = x.shape
    T = B * S
    depth = norm_w.shape[0]
    ffn_w_in = ffn_w_in.astype(BF16)
    ffn_w_out = ffn_w_out.astype(BF16)
    ret_w_in = ret_w_in.astype(BF16)
    ret_w_out = ret_w_out.astype(BF16)
    fox_w_in_b = fox_w_in.astype(BF16)
    fox_w_out = fox_w_out.astype(BF16)
    ple_w_proj = ple_w_proj.astype(BF16)
    ple_w_gate = ple_w_gate.astype(BF16)
    norm_w = norm_w.reshape(depth, 4, 1, D)
    final_w = final_norm_w.reshape(1, D)
    p = p.reshape(depth, T, D_PLE)

    cos, sin, gamma_c, decay, zeta, xi = _retention_tables(S)
    place = _fox_place()

    gn_w = ret_gn_w.reshape(-1, RET_HEADS, 1, RET_DV)
    h = x.reshape(T, D)
    for i in range(depth):
        j = i // 2
        h = _ffn(h, norm_w, ffn_w_in, ffn_w_out, i, 0)
        mix = None
        if i % 2 == 0:
            q, k, v, g = _ret_proj(h, norm_w, ret_w_in, i, j, cos, sin, S)
            h = _ret_core(q, k, v, g, h, gamma_c, decay, zeta, xi, gn_w, ret_w_out, j, B, S)
        else:
            w_f = jnp.pad(fox_w_in_b[j][:, 3 * D:], ((0, 0), (0, LANES - FOX_HEADS)))
            b_f = jnp.pad(fox_b_f[j], (0, LANES - FOX_HEADS)).reshape(1, LANES)
            qa, ka, vt = _fox_proj(h.reshape(B, S, D), norm_w, fox_w_in_b, i, j, w_f, b_f, place)
            mix = (_fox_attn(qa, ka, vt).reshape(T, D), fox_w_out, j)
        h = _ffn(h, norm_w, ffn_w_in, ffn_w_out, i, 1, mix=mix,
                 ple=(p, ple_w_gate, ple_w_proj),
                 final_w=final_w if i == depth - 1 else None)
    return h.reshape(B, S, D)
```

```python
import functools

import jax
import jax.numpy as jnp
from jax import lax
from jax.experimental import pallas as pl
from jax.experimental.pallas import tpu as pltpu

D_MODEL = 1024
D_PLE = 256
D_FF = 2816
EPS = 1e-6
RET_DK = 256
RET_HEADS = D_MODEL // RET_DK
RET_DV = 2 * RET_DK
RET_CHUNK = 256
ROPE_BASE = 10000.0
FOX_DH = 64
FOX_HEADS = D_MODEL // FOX_DH

LANES = 128
VMEM_LIMIT = 56 * 1024 * 1024

F32 = jnp.float32
BF16 = jnp.bfloat16

AUG_ONE = FOX_DH
AUG_C = FOX_DH + 3


def _rms(x, w):
    return x * lax.rsqrt(jnp.mean(x * x, axis=-1, keepdims=True) + EPS) * w


def _dot(a, b):
    return jnp.dot(a, b, preferred_element_type=F32)


def _dot_nt(a, b):
    return lax.dot_general(a, b, (((1,), (1,)), ((), ())), preferred_element_type=F32)


def _dot_tn(a, b):
    return lax.dot_general(a, b, (((0,), (0,)), ((), ())), preferred_element_type=F32)


def _resident(shape, index_map):
    return pl.BlockSpec(shape, index_map, pipeline_mode=pl.Buffered(1))


def _layer(arr, *lead, cols=None):
    tail = arr.shape[len(lead):]
    if cols is not None:
        tail = tail[:-1] + (cols,)
    idx = tuple(lead) + (0,) * len(tail)
    return _resident((None,) * len(lead) + tail, lambda *_: idx)


def _params(*sem):
    return pltpu.CompilerParams(dimension_semantics=sem, vmem_limit_bytes=VMEM_LIMIT)


FFN_TM = 1024
FFN_TF = 256


def _ffn_kernel(*refs, mix, ple, final):
    refs = list(refs)
    o_ref = refs.pop()
    x_ref, nw_ref, win_ref, wout_ref = refs[:4]
    rest = refs[4:]
    x = x_ref[...]
    if mix:
        a_ref, wmix_ref = rest[:2]
        rest = rest[2:]
        x = x + _dot(a_ref[...], wmix_ref[...])
    xn = _rms(x, nw_ref[...]).astype(BF16)
    acc = jnp.zeros_like(x)
    for c in range(D_FF // FFN_TF):
        lo = c * FFN_TF
        g = _dot(xn, win_ref[:, lo:lo + FFN_TF])
        u = _dot(xn, win_ref[:, D_FF + lo:D_FF + lo + FFN_TF])
        a = (g * jax.nn.sigmoid(g) * u).astype(BF16)
        acc = acc + _dot(a, wout_ref[lo:lo + FFN_TF, :])
    h = x + 0.5 * acc
    if ple:
        p_ref, gnw_ref, wgate_ref, wproj_ref = rest[:4]
        rest = rest[4:]
        hn = _rms(h, gnw_ref[...]).astype(BF16)
        gate = jax.nn.sigmoid(_dot(hn, wgate_ref[...]))
        h = h + gate * _dot(p_ref[...].astype(BF16), wproj_ref[...])
    if final:
        h = _rms(h, rest[0][...])
    o_ref[...] = h


def _ffn(h, norm_w, w_in, w_out, i, k, mix=None, ple=None, final_w=None):
    T = h.shape[0]
    tm = FFN_TM if mix is None and ple is None else FFN_TM // 2
    row = lambda t: (t, 0)
    in_specs = [
        pl.BlockSpec((tm, D_MODEL), row),
        _layer(norm_w, i, 2 * k),
        _layer(w_in, i, k),
        _layer(w_out, i, k),
    ]
    args = [h, norm_w, w_in, w_out]
    if mix is not None:
        a, wmix, j = mix
        in_specs += [pl.BlockSpec((tm, a.shape[1]), row), _layer(wmix, j)]
        args += [a, wmix]
    if ple is not None:
        p, wgate, wproj = ple
        in_specs += [
            pl.BlockSpec((None, tm, D_PLE), lambda t: (i, t, 0)),
            _layer(norm_w, i, 3),
            _layer(wgate, i),
            _layer(wproj, i),
        ]
        args += [p, norm_w, wgate, wproj]
    if final_w is not None:
        in_specs.append(_resident((1, D_MODEL), lambda t: (0, 0)))
        args.append(final_w)
    return pl.pallas_call(
        functools.partial(_ffn_kernel, mix=mix is not None, ple=ple is not None,
                          final=final_w is not None),
        grid=(T // tm,),
        in_specs=in_specs,
        out_specs=pl.BlockSpec((tm, D_MODEL), row),
        out_shape=jax.ShapeDtypeStruct((T, D_MODEL), F32),
        compiler_params=_params("parallel"),
        name="ffn",
    )(*args)


RETP_TM = 512
RET_QK = RET_HEADS * RET_DK
RET_V = RET_HEADS * RET_DV


def _ret_proj_kernel(x_ref, nw_ref, w_ref, cos_ref, sin_ref, q_ref, k_ref, v_ref, g_ref):
    xn = _rms(x_ref[...], nw_ref[...]).astype(BF16)
    cos = cos_ref[...]
    sin = sin_ref[...]
    half = RET_DK // 2

    def rotary(col, out_ref, scale):
        for h in range(RET_HEADS):
            y = _dot(xn, w_ref[:, col + h * RET_DK:col + (h + 1) * RET_DK])
            x1, x2 = y[:, :half], y[:, half:]
            lo = h * RET_DK
            out_ref[:, lo:lo + half] = ((x1 * cos - x2 * sin) * scale).astype(BF16)
            out_ref[:, lo + half:lo + RET_DK] = ((x1 * sin + x2 * cos) * scale).astype(BF16)

    rotary(0, q_ref, 1.0)
    rotary(RET_QK, k_ref, RET_DK ** -0.5)
    for h in range(RET_HEADS):
        lo = h * RET_DV
        v_ref[:, lo:lo + RET_DV] = _dot(xn, w_ref[:, 2 * RET_QK + lo:2 * RET_QK + lo + RET_DV]).astype(BF16)
        g_ref[:, lo:lo + RET_DV] = _dot(
            xn, w_ref[:, 2 * RET_QK + RET_V + lo:2 * RET_QK + RET_V + lo + RET_DV]).astype(BF16)


def _ret_proj(h, norm_w, w_in, i, j, cos, sin, seq):
    T = h.shape[0]
    row = lambda t: (t, 0)
    pos = lambda t: (t % (seq // RETP_TM), 0)
    return pl.pallas_call(
        _ret_proj_kernel,
        grid=(T // RETP_TM,),
        in_specs=[
            pl.BlockSpec((RETP_TM, D_MODEL), row),
            _layer(norm_w, i, 1),
            _layer(w_in, j),
            pl.BlockSpec((RETP_TM, RET_DK // 2), pos),
            pl.BlockSpec((RETP_TM, RET_DK // 2), pos),
        ],
        out_specs=[
            pl.BlockSpec((RETP_TM, RET_QK), row),
            pl.BlockSpec((RETP_TM, RET_QK), row),
            pl.BlockSpec((RETP_TM, RET_V), row),
            pl.BlockSpec((RETP_TM, RET_V), row),
        ],
        out_shape=[
            jax.ShapeDtypeStruct((T, RET_QK), BF16),
            jax.ShapeDtypeStruct((T, RET_QK), BF16),
            jax.ShapeDtypeStruct((T, RET_V), BF16),
            jax.ShapeDtypeStruct((T, RET_V), BF16),
        ],
        compiler_params=_params("parallel"),
        name="ret_proj",
    )(h, norm_w, w_in, cos, sin)


def _ret_core_kernel(gc_ref, q_ref, k_ref, v_ref, g_ref, res_ref, decay_ref, zeta_ref, xi_ref,
                     gn_ref, wo_ref, o_ref, state_ref):
    @pl.when(pl.program_id(1) == 0)
    def _():
        state_ref[...] = jnp.zeros_like(state_ref)

    heads = range(RET_HEADS)
    qs = [q_ref[:, h * RET_DK:(h + 1) * RET_DK] for h in heads]
    ks = [k_ref[:, h * RET_DK:(h + 1) * RET_DK] for h in heads]
    vs = [v_ref[:, h * RET_DV:(h + 1) * RET_DV] for h in heads]
    ss = [(_dot_nt(qs[h], ks[h]) * decay_ref[h]).astype(BF16) for h in heads]
    cross = [_dot(qs[h], state_ref[h].astype(BF16)) * xi_ref[h] for h in heads]
    outs = [_dot(ss[h], vs[h]) + cross[h] for h in heads]
    for h in heads:
        kz = (ks[h].astype(F32) * zeta_ref[h]).astype(BF16)
        state_ref[h] = gc_ref[h] * state_ref[h] + _dot_tn(kz, vs[h])
    acc = res_ref[...]
    for h in heads:
        gh = g_ref[:, h * RET_DV:(h + 1) * RET_DV].astype(F32)
        y = (gh * jax.nn.sigmoid(gh) * _rms(outs[h], gn_ref[h])).astype(BF16)
        acc = acc + _dot(y, wo_ref[h * RET_DV:(h + 1) * RET_DV, :])
    o_ref[...] = acc


def _ret_core(q, k, v, g, res, gamma_c, decay, zeta, xi, gn_w, w_out, j, batch, seq):
    T = q.shape[0]
    C = RET_CHUNK
    nc = seq // C
    row = lambda b, c: (b * nc + c, 0)
    fixed3 = lambda b, c: (0, 0, 0)
    return pl.pallas_call(
        _ret_core_kernel,
        grid=(batch, nc),
        in_specs=[
            pl.BlockSpec(memory_space=pltpu.SMEM),
            pl.BlockSpec((C, RET_QK), row),
            pl.BlockSpec((C, RET_QK), row),
            pl.BlockSpec((C, RET_V), row),
            pl.BlockSpec((C, RET_V), row),
            pl.BlockSpec((C, D_MODEL), row),
            _resident((RET_HEADS, C, C), fixed3),
            _resident((RET_HEADS, C, 1), fixed3),
            _resident((RET_HEADS, C, 1), fixed3),
            _layer(gn_w, j),
            _layer(w_out, j),
        ],
        out_specs=pl.BlockSpec((C, D_MODEL), row),
        out_shape=jax.ShapeDtypeStruct((T, D_MODEL), F32),
        scratch_shapes=[pltpu.VMEM((RET_HEADS, RET_DK, RET_DV), F32)],
        compiler_params=_params("parallel", "arbitrary"),
        name="ret_core",
    )(gamma_c, q, k, v, g, res, decay, zeta, xi, gn_w, w_out)


def _retention_tables(seq):
    H, C = RET_HEADS, RET_CHUNK
    half = RET_DK // 2
    pos = jnp.arange(seq, dtype=F32)
    inv_freq = ROPE_BASE ** (-jnp.arange(half, dtype=F32) / half)
    ang = pos[:, None] * inv_freq[None, :]
    log_gamma = jnp.log1p(-jnp.exp2(-5.0 - jnp.arange(H, dtype=F32)))
    idx = jnp.arange(C, dtype=F32)
    diff = idx[:, None] - idx[None, :]
    decay = jnp.where(diff[None] >= 0,
                      jnp.exp(log_gamma[:, None, None] * jnp.maximum(diff, 0.0)[None]), 0.0)
    zeta = jnp.exp(log_gamma[:, None] * (C - 1 - idx)[None, :])[:, :, None]
    xi = jnp.exp(log_gamma[:, None] * (idx + 1)[None, :])[:, :, None]
    gamma_c = jnp.exp(log_gamma * C)
    return jnp.cos(ang), jnp.sin(ang), gamma_c, decay, zeta, xi


FOXP_TM = 512
FOX_SUB = 256
FOX_NSUB = 4
FOX_TQ = FOX_NSUB * FOX_SUB
FOX_BK = 256
FOX_VT = FOX_DH + 16
LOG2E = 1.4426950408889634


def _cumsum_rows(x):
    n = x.shape[0]
    row = lax.broadcasted_iota(jnp.int32, x.shape, 0)
    shift = 1
    while shift < n:
        x = x + jnp.where(row >= shift, pltpu.roll(x, shift, axis=0), 0.0)
        shift *= 2
    return x


def _split3(c):
    hi = c.astype(BF16).astype(F32)
    r = c - hi
    mid = r.astype(BF16).astype(F32)
    lo = (r - mid).astype(BF16).astype(F32)
    return hi, mid, lo


def _fox_proj_kernel(x_ref, nw_ref, w_ref, wf_ref, bf_ref, place_ref,
                     q_ref, k_ref, vt_ref, carry_ref):
    @pl.when(pl.program_id(1) == 0)
    def _():
        carry_ref[...] = jnp.zeros_like(carry_ref)

    xn = _rms(x_ref[0], nw_ref[...]).astype(BF16)
    tm = xn.shape[0]

    z = _dot(xn, wf_ref[...]) + bf_ref[...]
    log_f = jnp.minimum(z, 0.0) - jnp.log(1.0 + jnp.exp(-jnp.abs(z)))
    c = _cumsum_rows(log_f) + carry_ref[...]
    carry_ref[...] = c[tm - 1:tm, :]

    hi, mid, lo = _split3(c * LOG2E)
    lane = lax.broadcasted_iota(jnp.int32, c.shape, 1)
    parts = jnp.where(lane < FOX_HEADS, hi,
                      jnp.where(lane < 2 * FOX_HEADS, pltpu.roll(mid, FOX_HEADS, axis=1),
                                pltpu.roll(lo, 2 * FOX_HEADS, axis=1))).astype(BF16)

    parts6 = _dot(parts, place_ref[...])

    dh = FOX_DH
    nqk = FOX_HEADS * dh
    lane_t = lax.broadcasted_iota(jnp.int32, (tm, LANES), 1)
    is_feat, is_one, is_c = lane_t < dh, lane_t < AUG_C, lane_t < AUG_C + 3
    yq_all = _dot(xn, w_ref[:, 0:nqk]) * (dh ** -0.5 * LOG2E)
    yk_all = _dot(xn, w_ref[:, nqk:2 * nqk])
    for h in range(FOX_HEADS):
        pair, hh = divmod(h, 2)
        c6 = pltpu.roll(parts6, (AUG_ONE - 6 * h) % LANES, axis=1)
        yq = yq_all[:, pair * LANES:(pair + 1) * LANES]
        yk = yk_all[:, pair * LANES:(pair + 1) * LANES]
        if hh == 1:
            yq, yk = pltpu.roll(yq, dh, axis=1), pltpu.roll(yk, dh, axis=1)
        aug_q = jnp.where(is_one, 1.0, jnp.where(is_c, c6, 0.0))
        aug_k = jnp.where(is_one, -c6, jnp.where(is_c, 1.0, 0.0))
        q_ref[0, h] = jnp.where(is_feat, yq, aug_q).astype(BF16)
        k_ref[0, h] = jnp.where(is_feat, yk, aug_k).astype(BF16)

    sub_t = lax.broadcasted_iota(jnp.int32, (FOX_VT - dh, tm), 0)
    v_aug = jnp.where(sub_t == 0, 1.0, 0.0).astype(BF16)
    y_all = _dot(xn, w_ref[:, 2 * nqk:3 * nqk])
    for pair in range(FOX_HEADS // 2):
        yt = y_all[:, pair * LANES:(pair + 1) * LANES].T
        for hh in range(2):
            vt_ref[0, 2 * pair + hh, 0:dh, :] = yt[hh * dh:(hh + 1) * dh, :].astype(BF16)
            vt_ref[0, 2 * pair + hh, dh:FOX_VT, :] = v_aug


def _fox_place():
    rows = jnp.arange(LANES)[:, None]
    cols = jnp.arange(LANES)[None, :]
    head, part = cols // 6, cols % 3
    return ((cols < 6 * FOX_HEADS) & (rows == part * FOX_HEADS + head)).astype(BF16)


def _fox_proj(h3, norm_w, w_in, i, j, w_f, b_f, place):
    B, S, _ = h3.shape
    H = FOX_HEADS
    fixed = lambda b, t: (0, 0)
    return pl.pallas_call(
        _fox_proj_kernel,
        grid=(B, S // FOXP_TM),
        in_specs=[
            pl.BlockSpec((1, FOXP_TM, D_MODEL), lambda b, t: (b, t, 0)),
            _layer(norm_w, i, 1),
            _layer(w_in, j, cols=3 * D_MODEL),
            _resident((D_MODEL, LANES), fixed),
            _resident((1, LANES), fixed),
            _resident((LANES, LANES), fixed),
        ],
        out_specs=[
            pl.BlockSpec((1, H, FOXP_TM, LANES), lambda b, t: (b, 0, t, 0)),
            pl.BlockSpec((1, H, FOXP_TM, LANES), lambda b, t: (b, 0, t, 0)),
            pl.BlockSpec((1, H, FOX_VT, FOXP_TM), lambda b, t: (b, 0, 0, t)),
        ],
        out_shape=[jax.ShapeDtypeStruct((B, H, S, LANES), BF16)] * 2
        + [jax.ShapeDtypeStruct((B, H, FOX_VT, S), BF16)],
        scratch_shapes=[pltpu.VMEM((1, LANES), F32)],
        compiler_params=_params("parallel", "arbitrary"),
        name="fox_proj",
    )(h3, norm_w, w_in, w_f, b_f, place)


def _fox_attn_kernel(q_ref, k_ref, vt_ref, o_ref, s_ref, m_ref, acc_ref):
    i = pl.program_id(2)
    sub, bk, n_sub = FOX_SUB, FOX_BK, FOX_NSUB
    chains = [(hh, r) for hh in range(2) for r in range(n_sub)]

    def key_tile(j):
        return pl.ds(pl.multiple_of(j * bk, bk), bk)

    def issue(slot, c, j):
        hh, r = chains[c]
        s_ref[slot, c] = _dot_nt(k_ref[0, hh, key_tile(j), :], q_ref[0, hh, r * sub:(r + 1) * sub, :])

    def consume(slot, c, j, masked):
        hh, _ = chains[c]
        s_t = s_ref[slot, c]
        if masked:
            key = lax.broadcasted_iota(jnp.int32, (bk, sub), 0)
            query = lax.broadcasted_iota(jnp.int32, (bk, sub), 1)
            s_t = jnp.where(key <= query, s_t, -jnp.inf)
        m = m_ref[c]
        m_new = jnp.maximum(m, jnp.max(s_t, axis=0, keepdims=True))
        p_t = jnp.exp2(s_t - m_new).astype(BF16)
        acc_ref[c] = jnp.exp2(m - m_new) * acc_ref[c] + _dot(vt_ref[0, hh, :, key_tile(j)], p_t)
        m_ref[c] = m_new

    m_ref[...] = jnp.full_like(m_ref, -jnp.inf)
    acc_ref[...] = jnp.zeros_like(acc_ref)

    for c in range(len(chains)):
        issue(0, c, 0)

    @pl.loop(0, i)
    def _(u):
        for t in range(n_sub):
            for c in range(len(chains)):
                issue((t + 1) % 2, c, n_sub * u + t + 1)
                consume(t % 2, c, n_sub * u + t, False)

    for t in range(n_sub):
        for c, (_, r) in enumerate(chains):
            if r > t:
                issue((t + 1) % 2, c, n_sub * i + t + 1)
            if r >= t:
                consume(t % 2, c, n_sub * i + t, masked=(r == t))

    for r in range(n_sub):
        out_t = [acc_ref[n_sub * hh + r, 0:FOX_DH, :] / acc_ref[n_sub * hh + r, FOX_DH:FOX_DH + 1, :]
                 for hh in range(2)]
        o_ref[0, r * sub:(r + 1) * sub, :] = jnp.concatenate(out_t, axis=0).T.astype(BF16)


def _fox_attn(q, k, vt):
    B, H, S, _ = q.shape
    assert FOX_BK == FOX_SUB and FOX_NSUB % 2 == 0
    n_chain = 2 * FOX_NSUB
    return pl.pallas_call(
        _fox_attn_kernel,
        grid=(B, H // 2, S // FOX_TQ),
        in_specs=[
            pl.BlockSpec((1, 2, FOX_TQ, LANES), lambda b, hp, i: (b, hp, i, 0)),
            pl.BlockSpec((1, 2, S, LANES), lambda b, hp, i: (b, hp, 0, 0)),
            pl.BlockSpec((1, 2, FOX_VT, S), lambda b, hp, i: (b, hp, 0, 0)),
        ],
        out_specs=pl.BlockSpec((1, FOX_TQ, LANES), lambda b, hp, i: (b, i, hp)),
        out_shape=jax.ShapeDtypeStruct((B, S, D_MODEL), BF16),
        scratch_shapes=[
            pltpu.VMEM((2, n_chain, FOX_BK, FOX_SUB), F32),
            pltpu.VMEM((n_chain, 1, FOX_SUB), F32),
            pltpu.VMEM((n_chain, FOX_VT, FOX_SUB), F32),
        ],
        compiler_params=_params("parallel", "parallel", "arbitrary"),
        name="fox_attn",
    )(q, k, vt)


def kernel(x, p, norm_w, ffn_w_in, ffn_w_out, ret_w_in, ret_gn_w, ret_w_out, fox_w_in, fox_b_f,
           fox_w_out, ple_w_proj, ple_w_gate, final_norm_w):
    B, S, D = x.shape
    T = B * S
    depth = norm_w.shape[0]
    ffn_w_in = ffn_w_in.astype(BF16)
    ffn_w_out = ffn_w_out.astype(BF16)
    ret_w_in = ret_w_in.astype(BF16)
    ret_w_out = ret_w_out.astype(BF16)
    fox_w_in_b = fox_w_in.astype(BF16)
    fox_w_out = fox_w_out.astype(BF16)
    ple_w_proj = ple_w_proj.astype(BF16)
    ple_w_gate = ple_w_gate.astype(BF16)
    norm_w = norm_w.reshape(depth, 4, 1, D)
    final_w = final_norm_w.reshape(1, D)
    p = p.reshape(depth, T, D_PLE)

    cos, sin, gamma_c, decay, zeta, xi = _retention_tables(S)
    place = _fox_place()

    gn_w = ret_gn_w.reshape(-1, RET_HEADS, 1, RET_DV)
    h = x.reshape(T, D)
    for i in range(depth):
        j = i // 2
        h = _ffn(h, norm_w, ffn_w_in, ffn_w_out, i, 0)
        mix = None
        if i % 2 == 0:
            q, k, v, g = _ret_proj(h, norm_w, ret_w_in, i, j, cos, sin, S)
            h = _ret_core(q, k, v, g, h, gamma_c, decay, zeta, xi, gn_w, ret_w_out, j, B, S)
        else:
            w_f = jnp.pad(fox_w_in_b[j][:, 3 * D:], ((0, 0), (0, LANES - FOX_HEADS)))
            b_f = jnp.pad(fox_b_f[j], (0, LANES - FOX_HEADS)).reshape(1, LANES)
            qa, ka, vt = _fox_proj(h.reshape(B, S, D), norm_w, fox_w_in_b, i, j, w_f, b_f, place)
            mix = (_fox_attn(qa, ka, vt).reshape(T, D), fox_w_out, j)
        h = _ffn(h, norm_w, ffn_w_in, ffn_w_out, i, 1, mix=mix,
                 ple=(p, ple_w_gate, ple_w_proj),
                 final_w=final_w if i == depth - 1 else None)
    return h.reshape(B, S, D)
```

```python
import functools

import jax
import jax.numpy as jnp
from jax import lax
from jax.experimental import pallas as pl
from jax.experimental.pallas import tpu as pltpu

D_MODEL = 1024
D_PLE = 256
D_FF = 2816
EPS = 1e-6
RET_DK = 256
RET_HEADS = D_MODEL // RET_DK
RET_DV = 2 * RET_DK
RET_CHUNK = 256
ROPE_BASE = 10000.0
FOX_DH = 64
FOX_HEADS = D_MODEL // FOX_DH

LANES = 128
VMEM_LIMIT = 56 * 1024 * 1024

F32 = jnp.float32
BF16 = jnp.bfloat16

AUG_ONE = FOX_DH
AUG_C = FOX_DH + 3


def _rms(x, w):
    return x * lax.rsqrt(jnp.mean(x * x, axis=-1, keepdims=True) + EPS) * w


def _dot(a, b):
    return jnp.dot(a, b, preferred_element_type=F32)


def _dot_nt(a, b):
    return lax.dot_general(a, b, (((1,), (1,)), ((), ())), preferred_element_type=F32)


def _dot_tn(a, b):
    return lax.dot_general(a, b, (((0,), (0,)), ((), ())), preferred_element_type=F32)


def _resident(shape, index_map):
    return pl.BlockSpec(shape, index_map, pipeline_mode=pl.Buffered(1))


def _layer(arr, *lead, cols=None):
    tail = arr.shape[len(lead):]
    if cols is not None:
        tail = tail[:-1] + (cols,)
    idx = tuple(lead) + (0,) * len(tail)
    return _resident((None,) * len(lead) + tail, lambda *_: idx)


def _params(*sem):
    return pltpu.CompilerParams(dimension_semantics=sem, vmem_limit_bytes=VMEM_LIMIT)


FFN_TM = 1024
FFN_TF = 256


def _ffn_kernel(*refs, mix, ple, final):
    refs = list(refs)
    o_ref = refs.pop()
    x_ref, nw_ref, win_ref, wout_ref = refs[:4]
    rest = refs[4:]
    x = x_ref[...]
    if mix:
        a_ref, wmix_ref = rest[:2]
        rest = rest[2:]
        x = x + _dot(a_ref[...], wmix_ref[...])
    xn = _rms(x, nw_ref[...]).astype(BF16)
    acc = jnp.zeros_like(x)
    for c in range(D_FF // FFN_TF):
        lo = c * FFN_TF
        g = _dot(xn, win_ref[:, lo:lo + FFN_TF])
        u = _dot(xn, win_ref[:, D_FF + lo:D_FF + lo + FFN_TF])
        a = (g * jax.nn.sigmoid(g) * u).astype(BF16)
        acc = acc + _dot(a, wout_ref[lo:lo + FFN_TF, :])
    h = x + 0.5 * acc
    if ple:
        p_ref, gnw_ref, wgate_ref, wproj_ref = rest[:4]
        rest = rest[4:]
        hn = _rms(h, gnw_ref[...]).astype(BF16)
        gate = jax.nn.sigmoid(_dot(hn, wgate_ref[...]))
        h = h + gate * _dot(p_ref[...].astype(BF16), wproj_ref[...])
    if final:
        h = _rms(h, rest[0][...])
    o_ref[...] = h


def _ffn(h, norm_w, w_in, w_out, i, k, mix=None, ple=None, final_w=None):
    T = h.shape[0]
    tm = FFN_TM if mix is None and ple is None else FFN_TM // 2
    row = lambda t: (t, 0)
    in_specs = [
        pl.BlockSpec((tm, D_MODEL), row),
        _layer(norm_w, i, 2 * k),
        _layer(w_in, i, k),
        _layer(w_out, i, k),
    ]
    args = [h, norm_w, w_in, w_out]
    if mix is not None:
        a, wmix, j = mix
        in_specs += [pl.BlockSpec((tm, a.shape[1]), row), _layer(wmix, j)]
        args += [a, wmix]
    if ple is not None:
        p, wgate, wproj = ple
        in_specs += [
            pl.BlockSpec((None, tm, D_PLE), lambda t: (i, t, 0)),
            _layer(norm_w, i, 3),
            _layer(wgate, i),
            _layer(wproj, i),
        ]
        args += [p, norm_w, wgate, wproj]
    if final_w is not None:
        in_specs.append(_resident((1, D_MODEL), lambda t: (0, 0)))
        args.append(final_w)
    return pl.pallas_call(
        functools.partial(_ffn_kernel, mix=mix is not None, ple=ple is not None,
                          final=final_w is not None),
        grid=(T // tm,),
        in_specs=in_specs,
        out_specs=pl.BlockSpec((tm, D_MODEL), row),
        out_shape=jax.ShapeDtypeStruct((T, D_MODEL), F32),
        compiler_params=_params("parallel"),
        name="ffn",
    )(*args)


RETP_TM = 512
RET_QK = RET_HEADS * RET_DK
RET_V = RET_HEADS * RET_DV


def _ret_proj_kernel(x_ref, nw_ref, w_ref, cos_ref, sin_ref, q_ref, k_ref, v_ref, g_ref):
    xn = _rms(x_ref[...], nw_ref[...]).astype(BF16)
    cos = cos_ref[...]
    sin = sin_ref[...]
    half = RET_DK // 2

    def rotary(col, out_ref, scale):
        for h in range(RET_HEADS):
            y = _dot(xn, w_ref[:, col + h * RET_DK:col + (h + 1) * RET_DK])
            x1, x2 = y[:, :half], y[:, half:]
            lo = h * RET_DK
            out_ref[:, lo:lo + half] = ((x1 * cos - x2 * sin) * scale).astype(BF16)
            out_ref[:, lo + half:lo + RET_DK] = ((x1 * sin + x2 * cos) * scale).astype(BF16)

    rotary(0, q_ref, 1.0)
    rotary(RET_QK, k_ref, RET_DK ** -0.5)
    for h in range(RET_HEADS):
        lo = h * RET_DV
        v_ref[:, lo:lo + RET_DV] = _dot(xn, w_ref[:, 2 * RET_QK + lo:2 * RET_QK + lo + RET_DV]).astype(BF16)
        g_ref[:, lo:lo + RET_DV] = _dot(
            xn, w_ref[:, 2 * RET_QK + RET_V + lo:2 * RET_QK + RET_V + lo + RET_DV]).astype(BF16)


def _ret_proj(h, norm_w, w_in, i, j, cos, sin, seq):
    T = h.shape[0]
    row = lambda t: (t, 0)
    pos = lambda t: (t % (seq // RETP_TM), 0)
    return pl.pallas_call(
        _ret_proj_kernel,
        grid=(T // RETP_TM,),
        in_specs=[
            pl.BlockSpec((RETP_TM, D_MODEL), row),
            _layer(norm_w, i, 1),
            _layer(w_in, j),
            pl.BlockSpec((RETP_TM, RET_DK // 2), pos),
            pl.BlockSpec((RETP_TM, RET_DK // 2), pos),
        ],
        out_specs=[
            pl.BlockSpec((RETP_TM, RET_QK), row),
            pl.BlockSpec((RETP_TM, RET_QK), row),
            pl.BlockSpec((RETP_TM, RET_V), row),
            pl.BlockSpec((RETP_TM, RET_V), row),
        ],
        out_shape=[
            jax.ShapeDtypeStruct((T, RET_QK), BF16),
            jax.ShapeDtypeStruct((T, RET_QK), BF16),
            jax.ShapeDtypeStruct((T, RET_V), BF16),
            jax.ShapeDtypeStruct((T, RET_V), BF16),
        ],
        compiler_params=_params("parallel"),
        name="ret_proj",
    )(h, norm_w, w_in, cos, sin)


def _ret_core_kernel(gc_ref, q_ref, k_ref, v_ref, g_ref, res_ref, decay_ref, zeta_ref, xi_ref,
                     gn_ref, wo_ref, o_ref, state_ref):
    @pl.when(pl.program_id(1) == 0)
    def _():
        state_ref[...] = jnp.zeros_like(state_ref)

    heads = range(RET_HEADS)
    qs = [q_ref[:, h * RET_DK:(h + 1) * RET_DK] for h in heads]
    ks = [k_ref[:, h * RET_DK:(h + 1) * RET_DK] for h in heads]
    vs = [v_ref[:, h * RET_DV:(h + 1) * RET_DV] for h in heads]
    ss = [(_dot_nt(qs[h], ks[h]) * decay_ref[h]).astype(BF16) for h in heads]
    cross = [_dot(qs[h], state_ref[h].astype(BF16)) * xi_ref[h] for h in heads]
    outs = [_dot(ss[h], vs[h]) + cross[h] for h in heads]
    for h in heads:
        kz = (ks[h].astype(F32) * zeta_ref[h]).astype(BF16)
        state_ref[h] = gc_ref[h] * state_ref[h] + _dot_tn(kz, vs[h])
    acc = res_ref[...]
    for h in heads:
        gh = g_ref[:, h * RET_DV:(h + 1) * RET_DV].astype(F32)
        y = (gh * jax.nn.sigmoid(gh) * _rms(outs[h], gn_ref[h])).astype(BF16)
        acc = acc + _dot(y, wo_ref[h * RET_DV:(h + 1) * RET_DV, :])
    o_ref[...] = acc


def _ret_core(q, k, v, g, res, gamma_c, decay, zeta, xi, gn_w, w_out, j, batch, seq):
    T = q.shape[0]
    C = RET_CHUNK
    nc = seq // C
    row = lambda b, c: (b * nc + c, 0)
    fixed3 = lambda b, c: (0, 0, 0)
    return pl.pallas_call(
        _ret_core_kernel,
        grid=(batch, nc),
        in_specs=[
            pl.BlockSpec(memory_space=pltpu.SMEM),
            pl.BlockSpec((C, RET_QK), row),
            pl.BlockSpec((C, RET_QK), row),
            pl.BlockSpec((C, RET_V), row),
            pl.BlockSpec((C, RET_V), row),
            pl.BlockSpec((C, D_MODEL), row),
            _resident((RET_HEADS, C, C), fixed3),
            _resident((RET_HEADS, C, 1), fixed3),
            _resident((RET_HEADS, C, 1), fixed3),
            _layer(gn_w, j),
            _layer(w_out, j),
        ],
        out_specs=pl.BlockSpec((C, D_MODEL), row),
        out_shape=jax.ShapeDtypeStruct((T, D_MODEL), F32),
        scratch_shapes=[pltpu.VMEM((RET_HEADS, RET_DK, RET_DV), F32)],
        compiler_params=_params("parallel", "arbitrary"),
        name="ret_core",
    )(gamma_c, q, k, v, g, res, decay, zeta, xi, gn_w, w_out)


def _retention_tables(seq):
    H, C = RET_HEADS, RET_CHUNK
    half = RET_DK // 2
    pos = jnp.arange(seq, dtype=F32)
    inv_freq = ROPE_BASE ** (-jnp.arange(half, dtype=F32) / half)
    ang = pos[:, None] * inv_freq[None, :]
    log_gamma = jnp.log1p(-jnp.exp2(-5.0 - jnp.arange(H, dtype=F32)))
    idx = jnp.arange(C, dtype=F32)
    diff = idx[:, None] - idx[None, :]
    decay = jnp.where(diff[None] >= 0,
                      jnp.exp(log_gamma[:, None, None] * jnp.maximum(diff, 0.0)[None]), 0.0)
    zeta = jnp.exp(log_gamma[:, None] * (C - 1 - idx)[None, :])[:, :, None]
    xi = jnp.exp(log_gamma[:, None] * (idx + 1)[None, :])[:, :, None]
    gamma_c = jnp.exp(log_gamma * C)
    return jnp.cos(ang), jnp.sin(ang), gamma_c, decay, zeta, xi


FOXP_TM = 512
FOX_SUB = 256
FOX_NSUB = 4
FOX_TQ = FOX_NSUB * FOX_SUB
FOX_BK = 256
FOX_VT = FOX_DH + 16
LOG2E = 1.4426950408889634


def _cumsum_rows(x):
    n = x.shape[0]
    row = lax.broadcasted_iota(jnp.int32, x.shape, 0)
    shift = 1
    while shift < n:
        x = x + jnp.where(row >= shift, pltpu.roll(x, shift, axis=0), 0.0)
        shift *= 2
    return x


def _split3(c):
    hi = c.astype(BF16).astype(F32)
    r = c - hi
    mid = r.astype(BF16).astype(F32)
    lo = (r - mid).astype(BF16).astype(F32)
    return hi, mid, lo


def _fox_proj_kernel(x_ref, nw_ref, w_ref, wf_ref, bf_ref, place_ref,
                     qt_ref, k_ref, vt_ref, carry_ref):
    @pl.when(pl.program_id(1) == 0)
    def _():
        carry_ref[...] = jnp.zeros_like(carry_ref)

    xn = _rms(x_ref[0], nw_ref[...]).astype(BF16)
    tm = xn.shape[0]

    z = _dot(xn, wf_ref[...]) + bf_ref[...]
    log_f = jnp.minimum(z, 0.0) - jnp.log(1.0 + jnp.exp(-jnp.abs(z)))
    c = _cumsum_rows(log_f) + carry_ref[...]
    carry_ref[...] = c[tm - 1:tm, :]

    hi, mid, lo = _split3(c * LOG2E)
    lane = lax.broadcasted_iota(jnp.int32, c.shape, 1)
    parts = jnp.where(lane < FOX_HEADS, hi,
                      jnp.where(lane < 2 * FOX_HEADS, pltpu.roll(mid, FOX_HEADS, axis=1),
                                pltpu.roll(lo, 2 * FOX_HEADS, axis=1))).astype(BF16)

    parts6 = _dot(parts, place_ref[...])

    dh = FOX_DH
    nqk = FOX_HEADS * dh
    lane_t = lax.broadcasted_iota(jnp.int32, (tm, LANES), 1)
    is_feat, is_one, is_c = lane_t < dh, lane_t < AUG_C, lane_t < AUG_C + 3
    yq_all = _dot(xn, w_ref[:, 0:nqk]) * (dh ** -0.5 * LOG2E)
    yk_all = _dot(xn, w_ref[:, nqk:2 * nqk])
    for h in range(FOX_HEADS):
        pair, hh = divmod(h, 2)
        c6 = pltpu.roll(parts6, (AUG_ONE - 6 * h) % LANES, axis=1)
        yq = yq_all[:, pair * LANES:(pair + 1) * LANES]
        yk = yk_all[:, pair * LANES:(pair + 1) * LANES]
        if hh == 1:
            yq, yk = pltpu.roll(yq, dh, axis=1), pltpu.roll(yk, dh, axis=1)
        aug_q = jnp.where(is_one, 1.0, jnp.where(is_c, c6, 0.0))
        aug_k = jnp.where(is_one, -c6, jnp.where(is_c, 1.0, 0.0))
        qt_ref[0, h] = jnp.where(is_feat, yq, aug_q).T.astype(BF16)
        k_ref[0, h] = jnp.where(is_feat, yk, aug_k).astype(BF16)

    sub_t = lax.broadcasted_iota(jnp.int32, (FOX_VT - dh, tm), 0)
    v_aug = jnp.where(sub_t == 0, 1.0, 0.0).astype(BF16)
    y_all = _dot(xn, w_ref[:, 2 * nqk:3 * nqk])
    for pair in range(FOX_HEADS // 2):
        yt = y_all[:, pair * LANES:(pair + 1) * LANES].T
        for hh in range(2):
            vt_ref[0, 2 * pair + hh, 0:dh, :] = yt[hh * dh:(hh + 1) * dh, :].astype(BF16)
            vt_ref[0, 2 * pair + hh, dh:FOX_VT, :] = v_aug


def _fox_place():
    rows = jnp.arange(LANES)[:, None]
    cols = jnp.arange(LANES)[None, :]
    head, part = cols // 6, cols % 3
    return ((cols < 6 * FOX_HEADS) & (rows == part * FOX_HEADS + head)).astype(BF16)


def _fox_proj(h3, norm_w, w_in, i, j, w_f, b_f, place):
    B, S, _ = h3.shape
    H = FOX_HEADS
    fixed = lambda b, t: (0, 0)
    return pl.pallas_call(
        _fox_proj_kernel,
        grid=(B, S // FOXP_TM),
        in_specs=[
            pl.BlockSpec((1, FOXP_TM, D_MODEL), lambda b, t: (b, t, 0)),
            _layer(norm_w, i, 1),
            _layer(w_in, j, cols=3 * D_MODEL),
            _resident((D_MODEL, LANES), fixed),
            _resident((1, LANES), fixed),
            _resident((LANES, LANES), fixed),
        ],
        out_specs=[
            pl.BlockSpec((1, H, LANES, FOXP_TM), lambda b, t: (b, 0, 0, t)),
            pl.BlockSpec((1, H, FOXP_TM, LANES), lambda b, t: (b, 0, t, 0)),
            pl.BlockSpec((1, H, FOX_VT, FOXP_TM), lambda b, t: (b, 0, 0, t)),
        ],
        out_shape=[jax.ShapeDtypeStruct((B, H, LANES, S), BF16),
                   jax.ShapeDtypeStruct((B, H, S, LANES), BF16),
                   jax.ShapeDtypeStruct((B, H, FOX_VT, S), BF16)],
        scratch_shapes=[pltpu.VMEM((1, LANES), F32)],
        compiler_params=_params("parallel", "arbitrary"),
        name="fox_proj",
    )(h3, norm_w, w_in, w_f, b_f, place)


def _fox_attn_kernel(qt_ref, k_ref, vt_ref, o_ref, s_ref, m_ref, acc_ref):
    i = pl.program_id(2)
    sub, bk, n_sub = FOX_SUB, FOX_BK, FOX_NSUB
    chains = [(hh, r) for hh in range(2) for r in range(n_sub)]

    def key_tile(j):
        return pl.ds(pl.multiple_of(j * bk, bk), bk)

    def issue(slot, c, j):
        hh, r = chains[c]
        s_ref[slot, c] = _dot(k_ref[0, hh, key_tile(j), :], qt_ref[0, hh, :, r * sub:(r + 1) * sub])

    def consume(slot, c, j, masked):
        hh, _ = chains[c]
        s_t = s_ref[slot, c]
        if masked:
            key = lax.broadcasted_iota(jnp.int32, (bk, sub), 0)
            query = lax.broadcasted_iota(jnp.int32, (bk, sub), 1)
            s_t = jnp.where(key <= query, s_t, -jnp.inf)
        m = m_ref[c]
        m_new = jnp.maximum(m, jnp.max(s_t, axis=0, keepdims=True))
        p_t = jnp.exp2(s_t - m_new).astype(BF16)
        acc_ref[c] = jnp.exp2(m - m_new) * acc_ref[c] + _dot(vt_ref[0, hh, :, key_tile(j)], p_t)
        m_ref[c] = m_new

    m_ref[...] = jnp.full_like(m_ref, -jnp.inf)
    acc_ref[...] = jnp.zeros_like(acc_ref)

    for c in range(len(chains)):
        issue(0, c, 0)

    @pl.loop(0, i)
    def _(u):
        for t in range(n_sub):
            for c in range(len(chains)):
                issue((t + 1) % 2, c, n_sub * u + t + 1)
                consume(t % 2, c, n_sub * u + t, False)

    for t in range(n_sub):
        for c, (_, r) in enumerate(chains):
            if r > t:
                issue((t + 1) % 2, c, n_sub * i + t + 1)
            if r >= t:
                consume(t % 2, c, n_sub * i + t, masked=(r == t))

    for r in range(n_sub):
        out_t = [acc_ref[n_sub * hh + r, 0:FOX_DH, :] / acc_ref[n_sub * hh + r, FOX_DH:FOX_DH + 1, :]
                 for hh in range(2)]
        o_ref[0, r * sub:(r + 1) * sub, :] = jnp.concatenate(out_t, axis=0).T.astype(BF16)


def _fox_attn(qt, k, vt):
    B, H, S, _ = k.shape
    assert FOX_BK == FOX_SUB and FOX_NSUB % 2 == 0
    n_chain = 2 * FOX_NSUB
    return pl.pallas_call(
        _fox_attn_kernel,
        grid=(B, H // 2, S // FOX_TQ),
        in_specs=[
            pl.BlockSpec((1, 2, LANES, FOX_TQ), lambda b, hp, i: (b, hp, 0, i)),
            pl.BlockSpec((1, 2, S, LANES), lambda b, hp, i: (b, hp, 0, 0)),
            pl.BlockSpec((1, 2, FOX_VT, S), lambda b, hp, i: (b, hp, 0, 0)),
        ],
        out_specs=pl.BlockSpec((1, FOX_TQ, LANES), lambda b, hp, i: (b, i, hp)),
        out_shape=jax.ShapeDtypeStruct((B, S, D_MODEL), BF16),
        scratch_shapes=[
            pltpu.VMEM((2, n_chain, FOX_BK, FOX_SUB), F32),
            pltpu.VMEM((n_chain, 1, FOX_SUB), F32),
            pltpu.VMEM((n_chain, FOX_VT, FOX_SUB), F32),
        ],
        compiler_params=_params("parallel", "parallel", "arbitrary"),
        name="fox_attn",
    )(qt, k, vt)


def kernel(x, p, norm_w, ffn_w_in, ffn_w_out, ret_w_in, ret_gn_w, ret_w_out, fox_w_in, fox_b_f,
           fox_w_out, ple_w_proj, ple_w_gate, final_norm_w):
    B, S, D = x.shape
    T = B * S
    depth = norm_w.shape[0]
    ffn_w_in = ffn_w_in.astype(BF16)
    ffn_w_out = ffn_w_out.astype(BF16)
    ret_w_in = ret_w_in.astype(BF16)
    ret_w_out = ret_w_out.astype(BF16)
    fox_w_in_b = fox_w_in.astype(BF16)
    fox_w_out = fox_w_out.astype(BF16)
    ple_w_proj = ple_w_proj.astype(BF16)
    ple_w_gate = ple_w_gate.astype(BF16)
    norm_w = norm_w.reshape(depth, 4, 1, D)
    final_w = final_norm_w.reshape(1, D)
    p = p.reshape(depth, T, D_PLE)

    cos, sin, gamma_c, decay, zeta, xi = _retention_tables(S)
    place = _fox_place()

    gn_w = ret_gn_w.reshape(-1, RET_HEADS, 1, RET_DV)
    h = x.reshape(T, D)
    for i in range(depth):
        j = i // 2
        h = _ffn(h, norm_w, ffn_w_in, ffn_w_out, i, 0)
        mix = None
        if i % 2 == 0:
            q, k, v, g = _ret_proj(h, norm_w, ret_w_in, i, j, cos, sin, S)
            h = _ret_core(q, k, v, g, h, gamma_c, decay, zeta, xi, gn_w, ret_w_out, j, B, S)
        else:
            w_f = jnp.pad(fox_w_in_b[j][:, 3 * D:], ((0, 0), (0, LANES - FOX_HEADS)))
            b_f = jnp.pad(fox_b_f[j], (0, LANES - FOX_HEADS)).reshape(1, LANES)
            qa, ka, vt = _fox_proj(h.reshape(B, S, D), norm_w, fox_w_in_b, i, j, w_f, b_f, place)
            mix = (_fox_attn(qa, ka, vt).reshape(T, D), fox_w_out, j)
        h = _ffn(h, norm_w, ffn_w_in, ffn_w_out, i, 1, mix=mix,
                 ple=(p, ple_w_gate, ple_w_proj),
                 final_w=final_w if i == depth - 1 else None)
    return h.reshape(B, S, D)
```
